```python
import math, functools
import jax, jax.numpy as jnp
from jax import lax
import numpy as np

D_MODEL = 1024
BATCH = 8
SEQ = 2048
DEPTH = 2
DEC_BATCH = 8
DEC_SEQ = 64
PAST_LEN = 2048

CHUNK = 64
QBLK = 128
HEAD_DIM = 64
FOX_HEADS = 6
FOX_WIDTH = FOX_HEADS * HEAD_DIM
CONV_CH = 256
CONV_WIDTH = 31
DIFF_HEADS = 6
DIFF_QK_DIM = 32
DIFF_V_DIM = 2 * DIFF_QK_DIM
DIFF_WIDTH = DIFF_HEADS * DIFF_V_DIM
MIX_WIDTH = FOX_WIDTH + CONV_CH + DIFF_WIDTH
SPLIT_SIZES = (FOX_WIDTH, FOX_WIDTH, FOX_WIDTH, FOX_HEADS, CONV_CH, CONV_CH, DIFF_WIDTH, DIFF_WIDTH, DIFF_WIDTH)
IN_COLS = 3 * FOX_WIDTH + FOX_HEADS + 2 * CONV_CH + 3 * DIFF_WIDTH
ROPE_THETA = 10000.0
PEER_HEADS = 8
PEER_QDIM = 256
N_KEYS = 128
N_EXPERTS = N_KEYS * N_KEYS
PEER_TOPK = 16
PEER_BLOCK = 64
EPS = 1e-6
NEG = -1e30

kernel_name = 'hymba_fox_conv_diff_peer_stream_step'


def rmsnorm(x, g):
    xf = x.astype(jnp.float32)
    y = xf * lax.rsqrt(jnp.mean(xf * xf, axis=-1, keepdims=True) + EPS)
    return (y * g.astype(jnp.float32)).astype(x.dtype)


def layernorm(x, g, b):
    xf = x.astype(jnp.float32)
    mu = jnp.mean(xf, axis=-1, keepdims=True)
    xc = xf - mu
    y = xc * lax.rsqrt(jnp.mean(xc * xc, axis=-1, keepdims=True) + EPS)
    return (y * g.astype(jnp.float32) + b.astype(jnp.float32)).astype(x.dtype)


def rope(x, pos):
    half = x.shape[-1] // 2
    inv = ROPE_THETA ** (-jnp.arange(half, dtype=jnp.float32) / half)
    ang = pos.astype(jnp.float32)[..., None] * inv
    cos = jnp.cos(ang)[:, :, None, :]
    sin = jnp.sin(ang)[:, :, None, :]
    xf = x.astype(jnp.float32)
    x1, x2 = xf[..., :half], xf[..., half:]
    return jnp.concatenate([x1 * cos - x2 * sin, x2 * cos + x1 * sin], axis=-1).astype(x.dtype)


def sweep_queries(fn, *qs):
    T = qs[0].shape[1]
    if T <= QBLK or T % QBLK:
        return fn(*qs)
    nb = T // QBLK
    blocks = tuple(jnp.moveaxis(a.reshape((a.shape[0], nb, QBLK) + a.shape[2:]), 1, 0) for a in qs)
    out = lax.map(lambda args: fn(*args), blocks)
    out = jnp.moveaxis(out, 0, 1)
    return out.reshape((out.shape[0], T) + out.shape[3:])


def fox_core(q, cq, qpos, k, v, ck, kpos):
    s = jnp.einsum('bqhd,bkhd->bhqk', q, k).astype(jnp.float32) * (HEAD_DIM ** -0.5)
    s = s + jnp.swapaxes(cq, 1, 2)[..., :, None] - jnp.swapaxes(ck, 1, 2)[..., None, :]
    mask = kpos[:, None, None, :] <= qpos[:, None, :, None]
    p = jax.nn.softmax(jnp.where(mask, s, NEG), axis=-1)
    return jnp.einsum('bhqk,bkhd->bqhd', p.astype(v.dtype), v)


def diff_core(q1, q2, qpos, k1, k2, v, kpos, lam):
    scale = DIFF_QK_DIM ** -0.5
    mask = (kpos // CHUNK)[:, None, None, :] <= (qpos // CHUNK)[:, None, :, None]
    s1 = jnp.einsum('bqhd,bkhd->bhqk', q1, k1).astype(jnp.float32) * scale
    s2 = jnp.einsum('bqhd,bkhd->bhqk', q2, k2).astype(jnp.float32) * scale
    p1 = jax.nn.softmax(jnp.where(mask, s1, NEG), axis=-1)
    p2 = jax.nn.softmax(jnp.where(mask, s2, NEG), axis=-1)
    a = p1 - lam * p2
    return jnp.einsum('bhqk,bkhd->bqhd', a.astype(v.dtype), v)


def causal_depthwise(buf, w, b):
    y = lax.conv_general_dilated(buf, w[:, None, :].astype(buf.dtype), window_strides=(1,), padding='VALID',
                                 dimension_numbers=('NWC', 'WIO', 'NWC'), feature_group_count=CONV_CH)
    return y + b.astype(y.dtype)


def peer_tokens(xt, w_q, sub_k1, sub_k2, u_tab, v_tab):
    n = xt.shape[0]
    q = (xt @ w_q).reshape(n, PEER_HEADS, 2, PEER_QDIM // 2)
    s1 = jnp.einsum('thd,nd->thn', q[:, :, 0], sub_k1).astype(jnp.float32)
    s2 = jnp.einsum('thd,nd->thn', q[:, :, 1], sub_k2).astype(jnp.float32)
    v1, i1 = lax.top_k(s1, PEER_TOPK)
    v2, i2 = lax.top_k(s2, PEER_TOPK)
    cand = (v1[..., :, None] + v2[..., None, :]).reshape(n, PEER_HEADS, PEER_TOPK * PEER_TOPK)
    top, idx = lax.top_k(cand, PEER_TOPK)
    e1 = jnp.take_along_axis(i1, idx // PEER_TOPK, axis=-1)
    e2 = jnp.take_along_axis(i2, idx % PEER_TOPK, axis=-1)
    eid = e1 * N_KEYS + e2
    g = jax.nn.softmax(top, axis=-1)
    act = jax.nn.gelu(jnp.einsum('td,thkd->thk', xt, u_tab[eid]).astype(jnp.float32))
    return jnp.einsum('thk,thkd->td', (g * act).astype(xt.dtype), v_tab[eid])


def peer_ffn(h, w_q, sub_k1, sub_k2, u_tab, v_tab):
    B, T, D = h.shape
    fn = functools.partial(peer_tokens, w_q=w_q, sub_k1=sub_k1, sub_k2=sub_k2, u_tab=u_tab, v_tab=v_tab)
    if T > PEER_BLOCK and T % PEER_BLOCK == 0:
        nb = T // PEER_BLOCK
        xt = h.reshape(B, nb, PEER_BLOCK, D).transpose(1, 0, 2, 3).reshape(nb, B * PEER_BLOCK, D)
        out = lax.map(fn, xt)
        return out.reshape(nb, B, PEER_BLOCK, D).transpose(1, 0, 2, 3).reshape(B, T, D)
    return fn(h.reshape(B * T, D)).reshape(B, T, D)


def trunk_layer(x, pos, past, layer_idx, norm1_g, w_in, b_forget, fox_q_g, fox_k_g, diff_q_g, diff_k_g,
                lam_q1, lam_k1, lam_q2, lam_k2, diff_sub_g, conv_w, conv_b, conv_ln_g, conv_ln_b, w_out,
                norm2_g, peer_w_q, peer_k1, peer_k2, peer_u, peer_v):
    B, T, _ = x.shape
    h = rmsnorm(x, norm1_g)
    proj = h @ w_in
    points = [int(p) for p in np.cumsum(SPLIT_SIZES)[:-1]]
    fq, fk, fv, ff, ga, gb, dq, dk, dv = jnp.split(proj, points, axis=-1)

    fq = rmsnorm(fq.reshape(B, T, FOX_HEADS, HEAD_DIM), fox_q_g)
    fk = rmsnorm(fk.reshape(B, T, FOX_HEADS, HEAD_DIM), fox_k_g)
    fv = fv.reshape(B, T, FOX_HEADS, HEAD_DIM)
    logf = jax.nn.log_sigmoid(ff.astype(jnp.float32) + b_forget.astype(jnp.float32)).astype(x.dtype)

    dq = rope(rmsnorm(dq.reshape(B, T, 2 * DIFF_HEADS, DIFF_QK_DIM), diff_q_g), pos).reshape(B, T, DIFF_HEADS, DIFF_V_DIM)
    dk = rope(rmsnorm(dk.reshape(B, T, 2 * DIFF_HEADS, DIFF_QK_DIM), diff_k_g), pos).reshape(B, T, DIFF_HEADS, DIFF_V_DIM)
    dv = dv.reshape(B, T, DIFF_HEADS, DIFF_V_DIM)

    u = ga * jax.nn.sigmoid(gb)

    if past is None:
        conv_hist = jnp.zeros((B, CONV_WIDTH - 1, CONV_CH), u.dtype)
        k_all, v_all, lf_all, dk_all, dv_all, kpos = fk, fv, logf, dk, dv, pos
    else:
        pk, pv, plf, pdk, pdv, conv_hist = past
        P = pk.shape[1]
        k_all = jnp.concatenate([pk, fk], axis=1)
        v_all = jnp.concatenate([pv, fv], axis=1)
        lf_all = jnp.concatenate([plf, logf], axis=1)
        dk_all = jnp.concatenate([pdk, dk], axis=1)
        dv_all = jnp.concatenate([pdv, dv], axis=1)
        kpos = jnp.concatenate([jnp.broadcast_to(jnp.arange(P, dtype=jnp.int32)[None], (B, P)), pos], axis=1)

    c_all = jnp.cumsum(lf_all.astype(jnp.float32), axis=1)
    cq = c_all[:, -T:]
    fox_out = sweep_queries(functools.partial(fox_core, k=k_all, v=v_all, ck=c_all, kpos=kpos), fq, cq, pos)

    lam_init = 0.8 - 0.6 * math.exp(-0.3 * layer_idx)
    lam = (jnp.exp(jnp.sum(lam_q1.astype(jnp.float32) * lam_k1.astype(jnp.float32)))
           - jnp.exp(jnp.sum(lam_q2.astype(jnp.float32) * lam_k2.astype(jnp.float32))) + lam_init)
    diff_out = sweep_queries(functools.partial(diff_core, k1=dk_all[..., :DIFF_QK_DIM], k2=dk_all[..., DIFF_QK_DIM:],
                                               v=dv_all, kpos=kpos, lam=lam),
                             dq[..., :DIFF_QK_DIM], dq[..., DIFF_QK_DIM:], pos)
    diff_out = rmsnorm(diff_out, diff_sub_g) * (1.0 - lam_init)

    buf = jnp.concatenate([conv_hist, u], axis=1)
    cy = jax.nn.silu(layernorm(causal_depthwise(buf, conv_w, conv_b), conv_ln_g, conv_ln_b))
    new_conv = buf[:, -(CONV_WIDTH - 1):]

    mixed = jnp.concatenate([fox_out.reshape(B, T, FOX_WIDTH), cy, diff_out.reshape(B, T, DIFF_WIDTH)], axis=-1)
    x = x + mixed @ w_out
    x = x + peer_ffn(rmsnorm(x, norm2_g), peer_w_q, peer_k1, peer_k2, peer_u, peer_v)
    return x, (fk, fv, logf, dk, dv, new_conv)


def setup_inputs(seed: int = 0) -> dict:
    key = jax.random.key(seed)
    ks = jax.random.split(key, 31)

    def nrm(k, shape, scale=1.0):
        return jax.random.normal(k, shape, jnp.float32) * scale

    def gain(k, shape):
        return 1.0 + 0.02 * jax.random.normal(k, shape, jnp.float32)

    return {
        'x_prompt': nrm(ks[0], (BATCH, SEQ, D_MODEL)),
        'x_sample': nrm(ks[1], (DEC_BATCH, DEC_SEQ, D_MODEL)),
        'cache_fox_k': nrm(ks[2], (DEPTH, DEC_BATCH, PAST_LEN, FOX_HEADS, HEAD_DIM)),
        'cache_fox_v': nrm(ks[3], (DEPTH, DEC_BATCH, PAST_LEN, FOX_HEADS, HEAD_DIM)),
        'cache_fox_logf': jax.nn.log_sigmoid(3.0 + nrm(ks[4], (DEPTH, DEC_BATCH, PAST_LEN, FOX_HEADS))),
        'cache_diff_k': nrm(ks[5], (DEPTH, DEC_BATCH, PAST_LEN, DIFF_HEADS, DIFF_V_DIM)),
        'cache_diff_v': nrm(ks[6], (DEPTH, DEC_BATCH, PAST_LEN, DIFF_HEADS, DIFF_V_DIM)),
        'state_conv': nrm(ks[7], (DEPTH, DEC_BATCH, CONV_WIDTH - 1, CONV_CH), 0.5),
        'norm1_g': gain(ks[8], (DEPTH, D_MODEL)),
        'w_in': nrm(ks[9], (DEPTH, D_MODEL, IN_COLS), D_MODEL ** -0.5),
        'b_forget': 3.0 + nrm(ks[10], (DEPTH, FOX_HEADS), 0.1),
        'fox_q_g': gain(ks[11], (DEPTH, HEAD_DIM)),
        'fox_k_g': gain(ks[12], (DEPTH, HEAD_DIM)),
        'diff_q_g': gain(ks[13], (DEPTH, DIFF_QK_DIM)),
        'diff_k_g': gain(ks[14], (DEPTH, DIFF_QK_DIM)),
        'lam_q1': nrm(ks[15], (DEPTH, DIFF_QK_DIM), 0.1),
        'lam_k1': nrm(ks[16], (DEPTH, DIFF_QK_DIM), 0.1),
        'lam_q2': nrm(ks[17], (DEPTH, DIFF_QK_DIM), 0.1),
        'lam_k2': nrm(ks[18], (DEPTH, DIFF_QK_DIM), 0.1),
        'diff_sub_g': gain(ks[19], (DEPTH, DIFF_V_DIM)),
        'conv_w': nrm(ks[20], (DEPTH, CONV_WIDTH, CONV_CH), CONV_WIDTH ** -0.5),
        'conv_b': nrm(ks[21], (DEPTH, CONV_CH), 0.02),
        'conv_ln_g': gain(ks[22], (DEPTH, CONV_CH)),
        'conv_ln_b': nrm(ks[23], (DEPTH, CONV_CH), 0.02),
        'w_out': nrm(ks[24], (DEPTH, MIX_WIDTH, D_MODEL), MIX_WIDTH ** -0.5),
        'norm2_g': gain(ks[25], (DEPTH, D_MODEL)),
        'peer_w_q': nrm(ks[26], (DEPTH, D_MODEL, PEER_HEADS * PEER_QDIM), D_MODEL ** -0.5),
        'peer_k1': nrm(ks[27], (DEPTH, N_KEYS, PEER_QDIM // 2), (PEER_QDIM // 2) ** -0.5),
        'peer_k2': nrm(ks[28], (DEPTH, N_KEYS, PEER_QDIM // 2), (PEER_QDIM // 2) ** -0.5),
        'peer_u': nrm(ks[29], (DEPTH, N_EXPERTS, D_MODEL), D_MODEL ** -0.5),
        'peer_v': nrm(ks[30], (DEPTH, N_EXPERTS, D_MODEL), PEER_HEADS ** -0.5),
    }


def reference(x_prompt, x_sample, cache_fox_k, cache_fox_v, cache_fox_logf, cache_diff_k, cache_diff_v, state_conv,
              norm1_g, w_in, b_forget, fox_q_g, fox_k_g, diff_q_g, diff_k_g, lam_q1, lam_k1, lam_q2, lam_k2,
              diff_sub_g, conv_w, conv_b, conv_ln_g, conv_ln_b, w_out, norm2_g, peer_w_q, peer_k1, peer_k2,
              peer_u, peer_v):
    Bp, Tp, _ = x_prompt.shape
    Bs, Ts, _ = x_sample.shape
    P = cache_fox_k.shape[2]
    pos_p = jnp.broadcast_to(jnp.arange(Tp, dtype=jnp.int32)[None], (Bp, Tp))
    pos_s = jnp.broadcast_to(P + jnp.arange(Ts, dtype=jnp.int32)[None], (Bs, Ts))

    xp, xs = x_prompt, x_sample
    p_st = [[] for _ in range(6)]
    s_st = [[] for _ in range(6)]
    for l in range(DEPTH):
        lp = (norm1_g[l], w_in[l], b_forget[l], fox_q_g[l], fox_k_g[l], diff_q_g[l], diff_k_g[l],
              lam_q1[l], lam_k1[l], lam_q2[l], lam_k2[l], diff_sub_g[l], conv_w[l], conv_b[l],
              conv_ln_g[l], conv_ln_b[l], w_out[l], norm2_g[l], peer_w_q[l], peer_k1[l], peer_k2[l],
              peer_u[l], peer_v[l])
        xp, stp = trunk_layer(xp, pos_p, None, l, *lp)
        past = (cache_fox_k[l], cache_fox_v[l], cache_fox_logf[l], cache_diff_k[l], cache_diff_v[l], state_conv[l])
        xs, sts = trunk_layer(xs, pos_s, past, l, *lp)
        for i in range(6):
            p_st[i].append(stp[i])
            s_st[i].append(sts[i])

    p_fox_k, p_fox_v, p_fox_logf, p_diff_k, p_diff_v, p_conv = [jnp.stack(a) for a in p_st]
    s_fox_k, s_fox_v, s_fox_logf, s_diff_k, s_diff_v, s_conv = [jnp.stack(a) for a in s_st]
    return (xp, xs, p_fox_k, p_fox_v, p_fox_logf, p_diff_k, p_diff_v, p_conv,
            s_fox_k, s_fox_v, s_fox_logf, s_diff_k, s_diff_v, s_conv)
```

```python
import functools
import math

import numpy as np
import jax
import jax.numpy as jnp
from jax import lax
from jax.experimental import pallas as pl
from jax.experimental.pallas import tpu as pltpu

F32 = jnp.float32
BF16 = jnp.bfloat16

D_MODEL = 1024
CHUNK = 64
HEAD_DIM = 64
FOX_HEADS = 6
FOX_WIDTH = FOX_HEADS * HEAD_DIM
CONV_CH = 256
CONV_WIDTH = 31
DIFF_HEADS = 6
DIFF_QK_DIM = 32
DIFF_V_DIM = 64
DIFF_WIDTH = DIFF_HEADS * DIFF_V_DIM
ROPE_THETA = 10000.0
PEER_HEADS = 8
PEER_QDIM = 256
N_KEYS = 128
N_EXPERTS = N_KEYS * N_KEYS
PEER_TOPK = 16
EPS = 1e-6
NEG = -1e30

LANES = 128
HEAD_PAIRS = FOX_WIDTH // LANES
ROW_TILE = 512
VMEM_LIMIT = 56 * 1024 * 1024

_SEG_FQ, _SEG_FK, _SEG_FV = 0, 384, 768
_SEG_GA, _SEG_GB = 1152, 1408
_SEG_DQ, _SEG_DK, _SEG_DV = 1664, 2048, 2432
_SEG_FF = 2816
IN_COLS_PAD = 2944


def _cparams(sem):
    return pltpu.CompilerParams(dimension_semantics=sem, vmem_limit_bytes=VMEM_LIMIT)


def _group_mean_sq(y, ones_ref, inv_n):
    y2 = y * y
    hi = y2.astype(BF16)
    lo = (y2 - hi.astype(F32)).astype(BF16)
    s = jnp.dot(hi, ones_ref[...], preferred_element_type=F32)
    s = s + jnp.dot(lo, ones_ref[...], preferred_element_type=F32)
    return s * inv_n


def _rope(y, cos, sin_signed):
    outs = []
    for c in range(y.shape[1] // LANES):
        sl = slice(c * LANES, (c + 1) * LANES)
        yc = y[:, sl]
        fwd = pltpu.roll(yc, LANES - DIFF_QK_DIM // 2, 1)
        bwd = pltpu.roll(yc, DIFF_QK_DIM // 2, 1)
        lane = lax.broadcasted_iota(jnp.int32, yc.shape, 1)
        partner = jnp.where((lane % DIFF_QK_DIM) < DIFF_QK_DIM // 2, fwd, bwd)
        outs.append(yc * cos[:, sl] + partner * sin_signed[:, sl])
    return jnp.concatenate(outs, axis=1)


def _inproj_kernel(x_ref, g1_ref, w_ref, bf_ref, gq_ref, gk_ref, dgq_ref, dgk_ref, cos_ref, sin_ref,
                   ones64_ref, ones32_ref,
                   fq_o, fk_o, fkb_o, fv_o, fvb_o, lf_o, u_o, dq_o, dk_o, dkb_o, dv_o, dvb_o):
    x = x_ref[...]
    ms = jnp.mean(x * x, axis=-1, keepdims=True)
    h = (x * lax.rsqrt(ms + EPS) * g1_ref[...]).astype(BF16)

    def seg(lo, width):
        return jnp.dot(h, w_ref[:, lo:lo + width], preferred_element_type=F32)

    fq = seg(_SEG_FQ, FOX_WIDTH)
    fq = fq * lax.rsqrt(_group_mean_sq(fq, ones64_ref, 1.0 / HEAD_DIM) + EPS) * gq_ref[...]
    fq_o[...] = (fq * (HEAD_DIM ** -0.5)).astype(BF16)

    fk = seg(_SEG_FK, FOX_WIDTH)
    fk = fk * lax.rsqrt(_group_mean_sq(fk, ones64_ref, 1.0 / HEAD_DIM) + EPS) * gk_ref[...]
    fk_o[...] = fk
    fkb_o[...] = fk.astype(BF16)

    fv = seg(_SEG_FV, FOX_WIDTH)
    fv_o[...] = fv
    fvb_o[...] = fv.astype(BF16)

    z = seg(_SEG_FF, LANES) + bf_ref[...]
    lf_o[...] = jnp.minimum(z, 0.0) - jnp.log1p(jnp.exp(-jnp.abs(z)))

    ga = seg(_SEG_GA, CONV_CH)
    gb = seg(_SEG_GB, CONV_CH)
    u_o[...] = ga * jax.nn.sigmoid(gb)

    cos = cos_ref[...]
    sin = sin_ref[...]
    dq = seg(_SEG_DQ, DIFF_WIDTH)
    dq = dq * lax.rsqrt(_group_mean_sq(dq, ones32_ref, 1.0 / DIFF_QK_DIM) + EPS) * dgq_ref[...]
    dq_o[...] = (_rope(dq, cos, sin) * (DIFF_QK_DIM ** -0.5)).astype(BF16)

    dk = seg(_SEG_DK, DIFF_WIDTH)
    dk = dk * lax.rsqrt(_group_mean_sq(dk, ones32_ref, 1.0 / DIFF_QK_DIM) + EPS) * dgk_ref[...]
    dk = _rope(dk, cos, sin)
    dk_o[...] = dk
    dkb_o[...] = dk.astype(BF16)

    dv = seg(_SEG_DV, DIFF_WIDTH)
    dv_o[...] = dv
    dvb_o[...] = dv.astype(BF16)


def _inproj(x, g1, w_pad, bf_pad, gq, gk, dgq, dgk, cos_tab, sin_tab, ones64, ones32, n_prompt_tiles,
            pos_tiles):
    n = x.shape[0]
    tm = ROW_TILE
    grid = (n // tm,)
    row = lambda w: pl.BlockSpec((tm, w), lambda i: (i, 0))
    full = lambda a: pl.BlockSpec(a.shape, lambda i: (0, 0))
    pos_map = lambda i: (jnp.where(i < n_prompt_tiles, i % pos_tiles, pos_tiles), 0)
    tab = pl.BlockSpec((tm, DIFF_WIDTH), pos_map)
    w3 = FOX_WIDTH
    out_shape = (
        jax.ShapeDtypeStruct((n, w3), BF16),
        jax.ShapeDtypeStruct((n, w3), F32),
        jax.ShapeDtypeStruct((n, w3), BF16),
        jax.ShapeDtypeStruct((n, w3), F32),
        jax.ShapeDtypeStruct((n, w3), BF16),
        jax.ShapeDtypeStruct((n, LANES), F32),
        jax.ShapeDtypeStruct((n, CONV_CH), F32),
        jax.ShapeDtypeStruct((n, w3), BF16),
        jax.ShapeDtypeStruct((n, w3), F32),
        jax.ShapeDtypeStruct((n, w3), BF16),
        jax.ShapeDtypeStruct((n, w3), F32),
        jax.ShapeDtypeStruct((n, w3), BF16),
    )
    out_specs = (row(w3), row(w3), row(w3), row(w3), row(w3), row(LANES), row(CONV_CH),
                 row(w3), row(w3), row(w3), row(w3), row(w3))
    return pl.pallas_call(
        _inproj_kernel,
        grid=grid,
        in_specs=[row(D_MODEL), full(g1), full(w_pad), full(bf_pad), full(gq), full(gk), full(dgq),
                  full(dgk), tab, tab, full(ones64), full(ones32)],
        out_specs=out_specs,
        out_shape=out_shape,
        compiler_params=_cparams(("parallel",)),
        name="inproj",
    )(x, g1, w_pad, bf_pad, gq, gk, dgq, dgk, cos_tab, sin_tab, ones64, ones32)


def _online_step(s, m, l, acc, v):
    m_new = jnp.maximum(m, jnp.max(s, axis=-1, keepdims=True))
    p = jnp.exp(s - m_new)
    alpha = jnp.exp(m - m_new)
    l = alpha * l + jnp.sum(p, axis=-1, keepdims=True)
    acc = alpha * acc + jnp.dot(p.astype(BF16), v, preferred_element_type=F32)
    return m_new, l, acc


def _fox_kernel(q_ref, k_ref, v_ref, ck_ref, o_ref, *, tq, tk, q_off):
    qi = pl.program_id(2)
    q = q_ref[0]
    lane = lax.broadcasted_iota(jnp.int32, (tq, LANES), 1)
    nkv = (q_off + (qi + 1) * tq + tk - 1) // tk
    qpos = lax.broadcasted_iota(jnp.int32, (tq, tk), 0) + (q_off + qi * tq)
    kcol = lax.broadcasted_iota(jnp.int32, (tq, tk), 1)
    first = lane < HEAD_DIM
    outs = []
    for j in range(2):
        qh = jnp.where(first if j == 0 else jnp.logical_not(first), q, jnp.zeros_like(q))

        def body(kb, carry, qh=qh, j=j):
            m, l, acc = carry
            start = pl.multiple_of(kb * tk, tk)
            k = k_ref[0, pl.ds(start, tk), :].astype(BF16)
            v = v_ref[0, pl.ds(start, tk), :].astype(BF16)
            s = lax.dot_general(qh, k, (((1,), (1,)), ((), ())), preferred_element_type=F32)
            s = s - ck_ref[0, 0, kb, j:j + 1, :]
            s = jnp.where(kcol + kb * tk <= qpos, s, NEG)
            return _online_step(s, m, l, acc, v)

        init = (jnp.full((tq, 1), NEG, F32), jnp.zeros((tq, 1), F32), jnp.zeros((tq, LANES), F32))
        m, l, acc = lax.fori_loop(0, nkv, body, init)
        outs.append(acc / l)
    o_ref[0] = jnp.where(first, outs[0], outs[1]).astype(o_ref.dtype)


def _diff_kernel(q_ref, k_ref, v_ref, lam_ref, g_ref, o_ref, *, tq, tk, q_off, out_scale):
    qi = pl.program_id(2)
    q = q_ref[0]
    lane = lax.broadcasted_iota(jnp.int32, (tq, LANES), 1)
    nkv = (q_off + (qi + 1) * tq + tk - 1) // tk
    qchunk = (lax.broadcasted_iota(jnp.int32, (tq, tk), 0) + (q_off + qi * tq)) // CHUNK
    kcol = lax.broadcasted_iota(jnp.int32, (tq, tk), 1)
    first = lane < DIFF_V_DIM
    lam = lam_ref[...]
    outs = []
    for j in range(2):
        qm = []
        for mp in range(2):
            lo = j * DIFF_V_DIM + mp * DIFF_QK_DIM
            sel = jnp.where(lane >= lo, lane, LANES) < lo + DIFF_QK_DIM
            qm.append(jnp.where(sel, q, jnp.zeros_like(q)))

        def body(kb, carry, qm=qm):
            m1, l1, a1, m2, l2, a2 = carry
            start = pl.multiple_of(kb * tk, tk)
            k = k_ref[0, pl.ds(start, tk), :].astype(BF16)
            v = v_ref[0, pl.ds(start, tk), :].astype(BF16)
            mask = (kcol + kb * tk) // CHUNK <= qchunk
            s1 = lax.dot_general(qm[0], k, (((1,), (1,)), ((), ())), preferred_element_type=F32)
            s2 = lax.dot_general(qm[1], k, (((1,), (1,)), ((), ())), preferred_element_type=F32)
            m1, l1, a1 = _online_step(jnp.where(mask, s1, NEG), m1, l1, a1, v)
            m2, l2, a2 = _online_step(jnp.where(mask, s2, NEG), m2, l2, a2, v)
            return m1, l1, a1, m2, l2, a2

        z1 = jnp.zeros((tq, 1), F32)
        za = jnp.zeros((tq, LANES), F32)
        ng = jnp.full((tq, 1), NEG, F32)
        m1, l1, a1, m2, l2, a2 = lax.fori_loop(0, nkv, body, (ng, z1, za, ng, z1, za))
        outs.append(a1 / l1 - lam * (a2 / l2))
    o = jnp.where(first, outs[0], outs[1])
    o2 = o * o
    ss0 = jnp.sum(jnp.where(first, o2, 0.0), axis=-1, keepdims=True)
    ss1 = jnp.sum(jnp.where(first, 0.0, o2), axis=-1, keepdims=True)
    ms = jnp.where(first, ss0, ss1) * (1.0 / DIFF_V_DIM)
    o = o * lax.rsqrt(ms + EPS) * g_ref[...]
    o_ref[0] = (o * out_scale).astype(o_ref.dtype)


def _attn_blocks(tq_total, tk_total):
    if tq_total == tk_total:
        t = min(256, tq_total)
        return t, t
    tk = tk_total
    for cand in (704, 512, 256, 192, 128, 64):
        if tk_total % cand == 0:
            tk = cand
            break
    return tq_total, tk


def _fox_attention(q, k, v, c_all):
    b, tq_total, _ = q.shape
    tk_total = k.shape[1]
    tq, tk = _attn_blocks(tq_total, tk_total)
    nkb = tk_total // tk
    ck = c_all.reshape(b, nkb, tk, HEAD_PAIRS, 2).transpose(0, 3, 1, 4, 2)
    kern = functools.partial(_fox_kernel, tq=tq, tk=tk, q_off=tk_total - tq_total)
    return pl.pallas_call(
        kern,
        grid=(b, HEAD_PAIRS, tq_total // tq),
        in_specs=[
            pl.BlockSpec((1, tq, LANES), lambda bi, hp, qi: (bi, qi, hp)),
            pl.BlockSpec((1, tk_total, LANES), lambda bi, hp, qi: (bi, 0, hp)),
            pl.BlockSpec((1, tk_total, LANES), lambda bi, hp, qi: (bi, 0, hp)),
            pl.BlockSpec((1, 1, nkb, 2, tk), lambda bi, hp, qi: (bi, hp, 0, 0, 0)),
        ],
        out_specs=pl.BlockSpec((1, tq, LANES), lambda bi, hp, qi: (bi, qi, hp)),
        out_shape=jax.ShapeDtypeStruct((b, tq_total, FOX_WIDTH), BF16),
        compiler_params=_cparams(("parallel", "parallel", "arbitrary")),
        name="fox_attn",
    )(q, k, v, ck)


def _diff_attention(q, k, v, lam_row, g_row, out_scale):
    b, tq_total, _ = q.shape
    tk_total = k.shape[1]
    tq, tk = _attn_blocks(tq_total, tk_total)
    kern = functools.partial(_diff_kernel, tq=tq, tk=tk, q_off=tk_total - tq_total, out_scale=out_scale)
    return pl.pallas_call(
        kern,
        grid=(b, HEAD_PAIRS, tq_total // tq),
        in_specs=[
            pl.BlockSpec((1, tq, LANES), lambda bi, hp, qi: (bi, qi, hp)),
            pl.BlockSpec((1, tk_total, LANES), lambda bi, hp, qi: (bi, 0, hp)),
            pl.BlockSpec((1, tk_total, LANES), lambda bi, hp, qi: (bi, 0, hp)),
            pl.BlockSpec((1, LANES), lambda bi, hp, qi: (0, 0)),
            pl.BlockSpec((1, LANES), lambda bi, hp, qi: (0, 0)),
        ],
        out_specs=pl.BlockSpec((1, tq, LANES), lambda bi, hp, qi: (bi, qi, hp)),
        out_shape=jax.ShapeDtypeStruct((b, tq_total, DIFF_WIDTH), BF16),
        compiler_params=_cparams(("parallel", "parallel", "arbitrary")),
        name="diff_attn",
    )(q, k, v, lam_row, g_row)


def _conv_kernel(buf_ref, w_ref, b_ref, g_ref, beta_ref, o_ref, *, t_total, tc):
    for c in range(t_total // tc):
        t0 = c * tc
        acc = jnp.zeros((tc, CONV_CH), F32)
        for w in range(CONV_WIDTH):
            acc = acc + buf_ref[0, t0 + w:t0 + w + tc, :] * w_ref[w:w + 1, :]
        y = acc + b_ref[...]
        mu = jnp.mean(y, axis=-1, keepdims=True)
        yc = y - mu
        var = jnp.mean(yc * yc, axis=-1, keepdims=True)
        yn = yc * lax.rsqrt(var + EPS) * g_ref[...] + beta_ref[...]
        o_ref[0, t0:t0 + tc, :] = (yn * jax.nn.sigmoid(yn)).astype(o_ref.dtype)


def _conv(buf, w, b, g, beta):
    bsz, tb, _ = buf.shape
    t_total = tb - (CONV_WIDTH - 1)
    tc = min(128, t_total)
    kern = functools.partial(_conv_kernel, t_total=t_total, tc=tc)
    full = lambda a: pl.BlockSpec(a.shape, lambda i: (0, 0))
    return pl.pallas_call(
        kern,
        grid=(bsz,),
        in_specs=[pl.BlockSpec((1, tb, CONV_CH), lambda i: (i, 0, 0)), full(w), full(b), full(g), full(beta)],
        out_specs=pl.BlockSpec((1, t_total, CONV_CH), lambda i: (i, 0, 0)),
        out_shape=jax.ShapeDtypeStruct((bsz, t_total, CONV_CH), BF16),
        compiler_params=_cparams(("parallel",)),
        name="conv",
    )(buf, w, b, g, beta)


def _outproj_kernel(x_ref, fo_ref, cy_ref, do_ref, w_ref, g2_ref, x1_o, hn_o):
    y = jnp.dot(fo_ref[...], w_ref[0:FOX_WIDTH, :], preferred_element_type=F32)
    y = y + jnp.dot(cy_ref[...], w_ref[FOX_WIDTH:FOX_WIDTH + CONV_CH, :], preferred_element_type=F32)
    y = y + jnp.dot(do_ref[...], w_ref[FOX_WIDTH + CONV_CH:, :], preferred_element_type=F32)
    x1 = x_ref[...] + y
    x1_o[...] = x1
    ms = jnp.mean(x1 * x1, axis=-1, keepdims=True)
    hn_o[...] = (x1 * lax.rsqrt(ms + EPS) * g2_ref[...]).astype(BF16)


def _outproj(x, fo, cy, do, w_out, g2):
    n = x.shape[0]
    tm = ROW_TILE
    row = lambda w: pl.BlockSpec((tm, w), lambda i: (i, 0))
    full = lambda a: pl.BlockSpec(a.shape, lambda i: (0, 0))
    return pl.pallas_call(
        _outproj_kernel,
        grid=(n // tm,),
        in_specs=[row(D_MODEL), row(FOX_WIDTH), row(CONV_CH), row(DIFF_WIDTH), full(w_out), full(g2)],
        out_specs=(row(D_MODEL), row(D_MODEL)),
        out_shape=(jax.ShapeDtypeStruct((n, D_MODEL), F32), jax.ShapeDtypeStruct((n, D_MODEL), BF16)),
        compiler_params=_cparams(("parallel",)),
        name="outproj",
    )(x, fo, cy, do, w_out, g2)


_STAIR = tuple(PEER_TOPK // (i + 1) for i in range(PEER_TOPK))


def _top_desc(s, rounds, scr):
    prev = None
    for r in range(rounds):
        cur = jnp.max(s if prev is None else jnp.where(s < prev, s, -jnp.inf), axis=0, keepdims=True)
        scr[r:r + 1, :] = cur
        prev = cur


def _peer_route_kernel(hn_ref, wq_ref, k1_ref, k2_ref, th_o, a_o, s2_o, b_o, v1_scr, v2_scr):
    tt = hn_ref.shape[0]
    q = jnp.dot(hn_ref[...], wq_ref[...], preferred_element_type=F32).astype(BF16)
    sub = lax.broadcasted_iota(jnp.int32, (8, tt), 0)
    for h in range(PEER_HEADS):
        q1 = q[:, h * PEER_QDIM:h * PEER_QDIM + N_KEYS]
        q2 = q[:, h * PEER_QDIM + N_KEYS:(h + 1) * PEER_QDIM]
        s1 = lax.dot_general(k1_ref[...], q1, (((1,), (1,)), ((), ())), preferred_element_type=F32)
        s2 = lax.dot_general(k2_ref[...], q2, (((1,), (1,)), ((), ())), preferred_element_type=F32)
        _top_desc(s1, PEER_TOPK + 1, v1_scr)
        _top_desc(s2, PEER_TOPK + 1, v2_scr)
        v2lo = v2_scr[0:8, :]
        cands = [v1_scr[0:1, :] + v2lo, v1_scr[0:1, :] + v2_scr[8:16, :]]
        for i in range(1, 8):
            cands.append(jnp.where(sub < _STAIR[i], v1_scr[i:i + 1, :] + v2lo, -jnp.inf))
        cands.append(v1_scr[8:16, :] + v2_scr[0:1, :])
        extra = jnp.where(sub == 0, v1_scr[16:17, :] + v2_scr[0:1, :],
                          jnp.where(sub == 1, v1_scr[0:1, :] + v2_scr[16:17, :], -jnp.inf))
        prev = None
        t16 = None
        for r in range(PEER_TOPK + 1):
            best = None
            for c in cands + [extra]:
                cm = c if prev is None else jnp.where(c < prev, c, -jnp.inf)
                best = cm if best is None else jnp.maximum(best, cm)
            prev = jnp.max(best, axis=0, keepdims=True)
            if r == PEER_TOPK - 1:
                t16 = prev
        t17 = prev
        top = v1_scr[0:1, :] + v2_scr[0:1, :]
        z = None
        for c in cands:
            e = jnp.sum(jnp.where(c >= t16, jnp.exp(c - top), 0.0), axis=0, keepdims=True)
            z = e if z is None else z + e
        tmid = 0.5 * (t16 + t17)
        rows = slice(h * N_KEYS, (h + 1) * N_KEYS)
        th_o[rows, :] = tmid - s1
        a_o[rows, :] = jnp.exp(s1 - v1_scr[0:1, :]) * (1.0 / z)
        s2_o[rows, :] = s2
        b_o[rows, :] = jnp.exp(s2 - v2_scr[0:1, :])


def _peer_route(hn, wq, k1, k2):
    n = hn.shape[0]
    tt = ROW_TILE
    rows = PEER_HEADS * N_KEYS
    full = lambda a: pl.BlockSpec(a.shape, lambda i: (0, 0))
    col = pl.BlockSpec((rows, tt), lambda i: (0, i))
    sds = jax.ShapeDtypeStruct((rows, n), F32)
    return pl.pallas_call(
        _peer_route_kernel,
        grid=(n // tt,),
        in_specs=[pl.BlockSpec((tt, D_MODEL), lambda i: (i, 0)), full(wq), full(k1), full(k2)],
        out_specs=(col, col, col, col),
        out_shape=(sds, sds, sds, sds),
        scratch_shapes=[pltpu.VMEM((PEER_TOPK + 8, tt), F32), pltpu.VMEM((PEER_TOPK + 8, tt), F32)],
        compiler_params=_cparams(("parallel",)),
        name="peer_route",
    )(hn, wq, k1, k2)


_GELU_C = math.sqrt(2.0 / math.pi)


def _gelu_tanh(x):
    return 0.5 * x * (1.0 + jnp.tanh(_GELU_C * (x + 0.044715 * (x * x * x))))


def _peer_dense_kernel(hn_ref, x1_ref, th_ref, a_ref, s2_ref, b_ref, u_ref, vt_ref, o_ref,
                       acc_scr, ht_scr, p_scr, *, eb):
    e = pl.program_id(1)
    tt = hn_ref.shape[0]
    n_e1 = eb // N_KEYS

    @pl.when(e == 0)
    def _():
        acc_scr[...] = jnp.zeros_like(acc_scr)

    ht_scr[...] = lax.dot_general(u_ref[...], hn_ref[...], (((1,), (1,)), ((), ())),
                                  preferred_element_type=F32)

    def per_e1(i, carry):
        e1 = e * n_e1 + i
        r0 = pl.multiple_of(i * N_KEYS, N_KEYS)
        for tg in range(tt // LANES):
            cs = slice(tg * LANES, (tg + 1) * LANES)
            th8 = th_ref[e1, :, cs]
            a8 = a_ref[e1, :, cs]
            w = jnp.zeros((N_KEYS, LANES), F32)
            for h in range(PEER_HEADS):
                hs = slice(h * N_KEYS, (h + 1) * N_KEYS)
                w = w + jnp.where(s2_ref[hs, cs] >= th8[h:h + 1, :], b_ref[hs, cs], 0.0) * a8[h:h + 1, :]
            g = _gelu_tanh(ht_scr[pl.ds(r0, N_KEYS), cs])
            p_scr[pl.ds(r0, N_KEYS), cs] = (w * g).astype(BF16)
        return carry

    lax.fori_loop(0, n_e1, per_e1, 0)
    acc_scr[...] += jnp.dot(vt_ref[...], p_scr[...], preferred_element_type=F32)

    @pl.when(e == pl.num_programs(1) - 1)
    def _():
        o_ref[...] = x1_ref[...] + acc_scr[...].T


def _peer_dense(hn, x1, th, a, s2, b, u_bf, vt_bf, eb=512):
    n = hn.shape[0]
    tt = ROW_TILE
    rows = PEER_HEADS * N_KEYS
    col = pl.BlockSpec((rows, tt), lambda i, e: (0, i))
    col_e1 = pl.BlockSpec((N_KEYS, PEER_HEADS, tt), lambda i, e: (0, 0, i))
    th = th.reshape(PEER_HEADS, N_KEYS, n).transpose(1, 0, 2)
    a = a.reshape(PEER_HEADS, N_KEYS, n).transpose(1, 0, 2)
    kern = functools.partial(_peer_dense_kernel, eb=eb)
    return pl.pallas_call(
        kern,
        grid=(n // tt, N_EXPERTS // eb),
        in_specs=[
            pl.BlockSpec((tt, D_MODEL), lambda i, e: (i, 0)),
            pl.BlockSpec((tt, D_MODEL), lambda i, e: (i, 0)),
            col_e1, col_e1, col, col,
            pl.BlockSpec((eb, D_MODEL), lambda i, e: (e, 0)),
            pl.BlockSpec((D_MODEL, eb), lambda i, e: (0, e)),
        ],
        out_specs=pl.BlockSpec((tt, D_MODEL), lambda i, e: (i, 0)),
        out_shape=jax.ShapeDtypeStruct((n, D_MODEL), F32),
        scratch_shapes=[pltpu.VMEM((D_MODEL, tt), F32), pltpu.VMEM((eb, tt), F32), pltpu.VMEM((eb, tt), BF16)],
        compiler_params=_cparams(("parallel", "arbitrary")),
        name="peer_dense",
    )(hn, x1, th, a, s2, b, u_bf, vt_bf)


def _pack_w_in(w):
    fq, fk, fv = w[:, 0:384], w[:, 384:768], w[:, 768:1152]
    ff = w[:, 1152:1158]
    ga, gb = w[:, 1158:1414], w[:, 1414:1670]
    dq, dk, dv = w[:, 1670:2054], w[:, 2054:2438], w[:, 2438:2822]
    ffp = jnp.pad(ff, ((0, 0), (0, LANES - FOX_HEADS)))
    return jnp.concatenate([fq, fk, fv, ga, gb, dq, dk, dv, ffp], axis=1).astype(BF16)


def _block_ones(width, group):
    idx = np.arange(width) // group
    return jnp.asarray((idx[:, None] == idx[None, :]).astype(np.float32), dtype=BF16)


def _rope_tables(positions):
    half = DIFF_QK_DIM // 2
    inv = ROPE_THETA ** (-jnp.arange(half, dtype=F32) / half)
    ang = positions.astype(F32)[:, None] * inv
    cos = jnp.cos(ang)
    sin = jnp.sin(ang)
    reps = DIFF_WIDTH // DIFF_QK_DIM
    cos_t = jnp.tile(jnp.concatenate([cos, cos], axis=1), (1, reps))
    sin_t = jnp.tile(jnp.concatenate([-sin, sin], axis=1), (1, reps))
    return cos_t, sin_t


def kernel(x_prompt, x_sample, cache_fox_k, cache_fox_v, cache_fox_logf, cache_diff_k, cache_diff_v, state_conv, norm1_g, w_in, b_forget, fox_q_g, fox_k_g, diff_q_g, diff_k_g, lam_q1, lam_k1, lam_q2, lam_k2, diff_sub_g, conv_w, conv_b, conv_ln_g, conv_ln_b, w_out, norm2_g, peer_w_q, peer_k1, peer_k2, peer_u, peer_v):
    bp, tp, d = x_prompt.shape
    bs, ts, _ = x_sample.shape
    depth = w_in.shape[0]
    past = cache_fox_k.shape[2]
    n_p = bp * tp
    n_s = bs * ts
    assert n_s == ROW_TILE and tp % ROW_TILE == 0 and d == D_MODEL

    x = jnp.concatenate([x_prompt.reshape(n_p, d), x_sample.reshape(n_s, d)], axis=0)

    pos_rows = jnp.concatenate([jnp.arange(tp, dtype=jnp.int32),
                                jnp.tile(past + jnp.arange(ts, dtype=jnp.int32), bs)])
    cos_tab, sin_tab = _rope_tables(pos_rows)
    ones64 = _block_ones(FOX_WIDTH, HEAD_DIM)
    ones32 = _block_ones(DIFF_WIDTH, DIFF_QK_DIM)
    row = lambda v, reps: jnp.tile(v.astype(F32), reps)[None, :]

    p_st = [[] for _ in range(6)]
    s_st = [[] for _ in range(6)]
    for l in range(depth):
        (fq, fk, fkb, fv, fvb, lf, u, dq, dk, dkb, dv, dvb) = _inproj(
            x, norm1_g[l][None, :], _pack_w_in(w_in[l]),
            jnp.pad(b_forget[l].astype(F32), (0, LANES - FOX_HEADS))[None, :],
            row(fox_q_g[l], FOX_HEADS), row(fox_k_g[l], FOX_HEADS),
            row(diff_q_g[l], 2 * DIFF_HEADS), row(diff_k_g[l], 2 * DIFF_HEADS),
            cos_tab, sin_tab, ones64, ones32, n_p // ROW_TILE, tp // ROW_TILE)

        lam_init = 0.8 - 0.6 * math.exp(-0.3 * l)
        lam = (jnp.exp(jnp.sum(lam_q1[l].astype(F32) * lam_k1[l].astype(F32)))
               - jnp.exp(jnp.sum(lam_q2[l].astype(F32) * lam_k2[l].astype(F32))) + lam_init)
        lam_row = jnp.full((1, LANES), lam, F32)
        gsub_row = row(diff_sub_g[l], 2)

        split3 = lambda a: (a[:n_p].reshape(bp, tp, -1), a[n_p:].reshape(bs, ts, -1))
        fq_p, fq_s = split3(fq)
        fk_p, fk_s = split3(fk)
        fkb_p, _ = split3(fkb)
        fv_p, fv_s = split3(fv)
        fvb_p, _ = split3(fvb)
        lf_p, lf_s = split3(lf[:, :FOX_HEADS])
        u_p, u_s = split3(u)
        dq_p, dq_s = split3(dq)
        dk_p, dk_s = split3(dk)
        dkb_p, _ = split3(dkb)
        dv_p, dv_s = split3(dv)
        dvb_p, _ = split3(dvb)

        c_p = jnp.cumsum(lf_p, axis=1)
        fox_p = _fox_attention(fq_p, fkb_p, fvb_p, c_p)
        diff_p = _diff_attention(dq_p, dkb_p, dvb_p, lam_row, gsub_row, 1.0 - lam_init)
        buf_p = jnp.concatenate([jnp.zeros((bp, CONV_WIDTH - 1, CONV_CH), F32), u_p], axis=1)

        k_all = jnp.concatenate([cache_fox_k[l].reshape(bs, past, FOX_WIDTH), fk_s], axis=1)
        v_all = jnp.concatenate([cache_fox_v[l].reshape(bs, past, FOX_WIDTH), fv_s], axis=1)
        c_s = jnp.cumsum(jnp.concatenate([cache_fox_logf[l].astype(F32), lf_s], axis=1), axis=1)
        dk_all = jnp.concatenate([cache_diff_k[l].reshape(bs, past, DIFF_WIDTH), dk_s], axis=1)
        dv_all = jnp.concatenate([cache_diff_v[l].reshape(bs, past, DIFF_WIDTH), dv_s], axis=1)
        fox_s = _fox_attention(fq_s, k_all, v_all, c_s)
        diff_s = _diff_attention(dq_s, dk_all, dv_all, lam_row, gsub_row, 1.0 - lam_init)
        buf_s = jnp.concatenate([state_conv[l].astype(F32), u_s], axis=1)

        cw = conv_w[l].astype(F32)
        conv_args = (cw, conv_b[l][None, :], conv_ln_g[l][None, :], conv_ln_b[l][None, :])
        cy_p = _conv(buf_p, *conv_args)
        cy_s = _conv(buf_s, *conv_args)

        join = lambda a, b: jnp.concatenate([a.reshape(n_p, -1), b.reshape(n_s, -1)], axis=0)
        x1, hn = _outproj(x, join(fox_p, fox_s), join(cy_p, cy_s), join(diff_p, diff_s),
                          w_out[l].astype(BF16), norm2_g[l][None, :])

        th, a, s2, b = _peer_route(hn, peer_w_q[l].astype(BF16), peer_k1[l].astype(BF16),
                                   peer_k2[l].astype(BF16))
        x = _peer_dense(hn, x1, th, a, s2, b, peer_u[l].astype(BF16), peer_v[l].T.astype(BF16))

        hd = lambda a, nh: a.reshape(a.shape[0], a.shape[1], nh, -1)
        for lst, vals in ((p_st, (hd(fk_p, FOX_HEADS), hd(fv_p, FOX_HEADS), lf_p, hd(dk_p, DIFF_HEADS),
                                  hd(dv_p, DIFF_HEADS), buf_p[:, -(CONV_WIDTH - 1):])),
                          (s_st, (hd(fk_s, FOX_HEADS), hd(fv_s, FOX_HEADS), lf_s, hd(dk_s, DIFF_HEADS),
                                  hd(dv_s, DIFF_HEADS), buf_s[:, -(CONV_WIDTH - 1):]))):
            for i in range(6):
                lst[i].append(vals[i])

    y_p = x[:n_p].reshape(bp, tp, d)
    y_s = x[n_p:].reshape(bs, ts, d)
    return (y_p, y_s) + tuple(jnp.stack(a) for a in p_st) + tuple(jnp.stack(a) for a in s_st)
```

```python
import functools
import math

import numpy as np
import jax
import jax.numpy as jnp
from jax import lax
from jax.experimental import pallas as pl
from jax.experimental.pallas import tpu as pltpu

F32 = jnp.float32
BF16 = jnp.bfloat16

D_MODEL = 1024
CHUNK = 64
HEAD_DIM = 64
FOX_HEADS = 6
FOX_WIDTH = FOX_HEADS * HEAD_DIM
CONV_CH = 256
CONV_WIDTH = 31
DIFF_HEADS = 6
DIFF_QK_DIM = 32
DIFF_V_DIM = 64
DIFF_WIDTH = DIFF_HEADS * DIFF_V_DIM
ROPE_THETA = 10000.0
PEER_HEADS = 8
PEER_QDIM = 256
N_KEYS = 128
N_EXPERTS = N_KEYS * N_KEYS
PEER_TOPK = 16
EPS = 1e-6
NEG = -1e30

LANES = 128
SUBLANES = 8
PACK = 16
HEAD_PAIRS = FOX_WIDTH // LANES
ROW_TILE = 512
ATTN_Q_BLOCK = 256
EXPERT_BLOCK = SUBLANES * N_KEYS
VMEM_LIMIT = 56 * 1024 * 1024

_SEG_FQ, _SEG_FK, _SEG_FV = 0, 384, 768
_SEG_GA, _SEG_GB = 1152, 1408
_SEG_DQ, _SEG_DK, _SEG_DV = 1664, 2048, 2432
_SEG_FF = 2816
IN_COLS_PAD = 2944

_NT = (((1,), (1,)), ((), ()))


def _cparams(sem):
    return pltpu.CompilerParams(dimension_semantics=sem, vmem_limit_bytes=VMEM_LIMIT)


def _group_mean_sq(y, ones_ref, inv_n):
    y2 = y * y
    hi = y2.astype(BF16)
    lo = (y2 - hi.astype(F32)).astype(BF16)
    s = jnp.dot(hi, ones_ref[...], preferred_element_type=F32)
    s = s + jnp.dot(lo, ones_ref[...], preferred_element_type=F32)
    return s * inv_n


def _rope(y, cos, sin_signed):
    outs = []
    for c in range(y.shape[1] // LANES):
        sl = slice(c * LANES, (c + 1) * LANES)
        yc = y[:, sl]
        fwd = pltpu.roll(yc, LANES - DIFF_QK_DIM // 2, 1)
        bwd = pltpu.roll(yc, DIFF_QK_DIM // 2, 1)
        lane = lax.broadcasted_iota(jnp.int32, yc.shape, 1)
        partner = jnp.where((lane % DIFF_QK_DIM) < DIFF_QK_DIM // 2, fwd, bwd)
        outs.append(yc * cos[:, sl] + partner * sin_signed[:, sl])
    return jnp.concatenate(outs, axis=1)


def _inproj_kernel(x_ref, g1_ref, w_ref, bf_ref, gq_ref, gk_ref, dgq_ref, dgk_ref, cos_ref, sin_ref,
                   ones64_ref, ones32_ref,
                   fq_o, fk_o, fkb_o, fv_o, fvb_o, lf_o, u_o, dq_o, dk_o, dkb_o, dv_o, dvb_o):
    x = x_ref[...]
    ms = jnp.mean(x * x, axis=-1, keepdims=True)
    h = (x * lax.rsqrt(ms + EPS) * g1_ref[...]).astype(BF16)

    def seg(lo, width):
        return jnp.dot(h, w_ref[:, lo:lo + width], preferred_element_type=F32)

    fq = seg(_SEG_FQ, FOX_WIDTH)
    fq = fq * lax.rsqrt(_group_mean_sq(fq, ones64_ref, 1.0 / HEAD_DIM) + EPS) * gq_ref[...]
    fq_o[...] = (fq * (HEAD_DIM ** -0.5)).astype(BF16)

    fk = seg(_SEG_FK, FOX_WIDTH)
    fk = fk * lax.rsqrt(_group_mean_sq(fk, ones64_ref, 1.0 / HEAD_DIM) + EPS) * gk_ref[...]
    fk_o[...] = fk
    fkb_o[...] = fk.astype(BF16)

    fv = seg(_SEG_FV, FOX_WIDTH)
    fv_o[...] = fv
    fvb_o[...] = fv.astype(BF16)

    z = seg(_SEG_FF, LANES) + bf_ref[...]
    lf_o[...] = jnp.minimum(z, 0.0) - jnp.log1p(jnp.exp(-jnp.abs(z)))

    ga = seg(_SEG_GA, CONV_CH)
    gb = seg(_SEG_GB, CONV_CH)
    u_o[...] = ga * jax.nn.sigmoid(gb)

    cos = cos_ref[...]
    sin = sin_ref[...]
    dq = seg(_SEG_DQ, DIFF_WIDTH)
    dq = dq * lax.rsqrt(_group_mean_sq(dq, ones32_ref, 1.0 / DIFF_QK_DIM) + EPS) * dgq_ref[...]
    dq_o[...] = (_rope(dq, cos, sin) * (DIFF_QK_DIM ** -0.5)).astype(BF16)

    dk = seg(_SEG_DK, DIFF_WIDTH)
    dk = dk * lax.rsqrt(_group_mean_sq(dk, ones32_ref, 1.0 / DIFF_QK_DIM) + EPS) * dgk_ref[...]
    dk = _rope(dk, cos, sin)
    dk_o[...] = dk
    dkb_o[...] = dk.astype(BF16)

    dv = seg(_SEG_DV, DIFF_WIDTH)
    dv_o[...] = dv
    dvb_o[...] = dv.astype(BF16)


def _inproj(x, g1, w_pad, bf_pad, gq, gk, dgq, dgk, cos_tab, sin_tab, ones64, ones32):
    n = x.shape[0]
    tm = ROW_TILE
    pos_tiles = cos_tab.shape[0] // tm
    row = lambda w: pl.BlockSpec((tm, w), lambda i: (i, 0))
    full = lambda a: pl.BlockSpec(a.shape, lambda i: (0, 0))
    tab = pl.BlockSpec((tm, DIFF_WIDTH), lambda i: (i % pos_tiles, 0))
    w3 = FOX_WIDTH
    out_shape = (
        jax.ShapeDtypeStruct((n, w3), BF16),
        jax.ShapeDtypeStruct((n, w3), F32),
        jax.ShapeDtypeStruct((n, w3), BF16),
        jax.ShapeDtypeStruct((n, w3), F32),
        jax.ShapeDtypeStruct((n, w3), BF16),
        jax.ShapeDtypeStruct((n, LANES), F32),
        jax.ShapeDtypeStruct((n, CONV_CH), F32),
        jax.ShapeDtypeStruct((n, w3), BF16),
        jax.ShapeDtypeStruct((n, w3), F32),
        jax.ShapeDtypeStruct((n, w3), BF16),
        jax.ShapeDtypeStruct((n, w3), F32),
        jax.ShapeDtypeStruct((n, w3), BF16),
    )
    out_specs = (row(w3), row(w3), row(w3), row(w3), row(w3), row(LANES), row(CONV_CH),
                 row(w3), row(w3), row(w3), row(w3), row(w3))
    return pl.pallas_call(
        _inproj_kernel,
        grid=(n // tm,),
        in_specs=[row(D_MODEL), full(g1), full(w_pad), full(bf_pad), full(gq), full(gk), full(dgq),
                  full(dgk), tab, tab, full(ones64), full(ones32)],
        out_specs=out_specs,
        out_shape=out_shape,
        compiler_params=_cparams(("parallel",)),
        name="inproj",
    )(x, g1, w_pad, bf_pad, gq, gk, dgq, dgk, cos_tab, sin_tab, ones64, ones32)


def _softmax_pv(s_parts, v_parts):
    m = None
    for s in s_parts:
        mi = jnp.max(s, axis=-1, keepdims=True)
        m = mi if m is None else jnp.maximum(m, mi)
    l = None
    acc = None
    for s, v in zip(s_parts, v_parts):
        p = jnp.exp(s - m)
        li = jnp.sum(p, axis=-1, keepdims=True)
        ai = jnp.dot(p.astype(BF16), v, preferred_element_type=F32)
        l = li if l is None else l + li
        acc = ai if acc is None else acc + ai
    return acc / l


def _fox_kernel(q_ref, kp_ref, vp_ref, kn_ref, vn_ref, ck_ref, o_ref, *, tq, n_q, past0):
    lane = lax.broadcasted_iota(jnp.int32, (tq, LANES), 1)
    first = lane < HEAD_DIM
    causal = (lax.broadcasted_iota(jnp.int32, (tq, tq), 1) <= lax.broadcasted_iota(jnp.int32, (tq, tq), 0))
    for qi in range(n_q):
        past = past0 + qi * tq
        rows = slice(qi * tq, (qi + 1) * tq)
        q = q_ref[0, rows, :]
        kd = kn_ref[0, rows, :].astype(BF16)
        vd = vn_ref[0, rows, :].astype(BF16)
        if past:
            kp = kp_ref[0, 0:past, :].astype(BF16)
            vp = vp_ref[0, 0:past, :].astype(BF16)
        outs = []
        for j in range(2):
            qh = jnp.where(first if j == 0 else jnp.logical_not(first), q, jnp.zeros_like(q))
            s_parts, v_parts = [], []
            if past:
                s = lax.dot_general(qh, kp, _NT, preferred_element_type=F32)
                s_parts.append(s - ck_ref[0, 0, j:j + 1, 0:past])
                v_parts.append(vp)
            s = lax.dot_general(qh, kd, _NT, preferred_element_type=F32)
            s = s - ck_ref[0, 0, j:j + 1, past:past + tq]
            s_parts.append(jnp.where(causal, s, NEG))
            v_parts.append(vd)
            outs.append(_softmax_pv(s_parts, v_parts))
        o_ref[0, rows, :] = jnp.where(first, outs[0], outs[1]).astype(o_ref.dtype)


def _diff_kernel(q_ref, kp_ref, vp_ref, kn_ref, vn_ref, lam_ref, g_ref, o_ref, *, tq, n_q, past0, out_scale):
    lane = lax.broadcasted_iota(jnp.int32, (tq, LANES), 1)
    first = lane < DIFF_V_DIM
    chunk_ok = (lax.broadcasted_iota(jnp.int32, (tq, tq), 1) // CHUNK
                <= lax.broadcasted_iota(jnp.int32, (tq, tq), 0) // CHUNK)
    lam = lam_ref[...]
    for qi in range(n_q):
        past = past0 + qi * tq
        rows = slice(qi * tq, (qi + 1) * tq)
        q = q_ref[0, rows, :]
        kd = kn_ref[0, rows, :].astype(BF16)
        vd = vn_ref[0, rows, :].astype(BF16)
        if past:
            kp = kp_ref[0, 0:past, :].astype(BF16)
            vp = vp_ref[0, 0:past, :].astype(BF16)
        outs = []
        for j in range(2):
            maps = []
            for mp in range(2):
                lo = j * DIFF_V_DIM + mp * DIFF_QK_DIM
                sel = jnp.where(lane >= lo, lane, LANES) < lo + DIFF_QK_DIM
                qm = jnp.where(sel, q, jnp.zeros_like(q))
                s_parts, v_parts = [], []
                if past:
                    s_parts.append(lax.dot_general(qm, kp, _NT, preferred_element_type=F32))
                    v_parts.append(vp)
                s = lax.dot_general(qm, kd, _NT, preferred_element_type=F32)
                s_parts.append(jnp.where(chunk_ok, s, NEG))
                v_parts.append(vd)
                maps.append(_softmax_pv(s_parts, v_parts))
            outs.append(maps[0] - lam * maps[1])
        o = jnp.where(first, outs[0], outs[1])
        o2 = o * o
        ss0 = jnp.sum(jnp.where(first, o2, 0.0), axis=-1, keepdims=True)
        ss1 = jnp.sum(jnp.where(first, 0.0, o2), axis=-1, keepdims=True)
        ms = jnp.where(first, ss0, ss1) * (1.0 / DIFF_V_DIM)
        o = o * lax.rsqrt(ms + EPS) * g_ref[...]
        o_ref[0, rows, :] = (o * out_scale).astype(o_ref.dtype)


def _attn_specs(q, k_past, k_new):
    b, tq_total, _ = q.shape
    tq = min(ATTN_Q_BLOCK, tq_total)
    same = k_past is k_new
    past0 = 0 if same else k_past.shape[1]
    blk = lambda t: pl.BlockSpec((1, t, LANES), lambda bi, hp: (bi, 0, hp))
    specs = [blk(tq_total), blk(k_past.shape[1]), blk(k_past.shape[1]), blk(tq_total), blk(tq_total)]
    return b, tq_total, tq, past0, specs, blk(tq_total)


def _fox_attention(q, k_past, v_past, k_new, v_new, c_all):
    b, tq_total, tq, past0, specs, ospec = _attn_specs(q, k_past, k_new)
    tk_total = c_all.shape[1]
    ck = c_all.transpose(0, 2, 1).reshape(b, HEAD_PAIRS, 2, tk_total)
    kern = functools.partial(_fox_kernel, tq=tq, n_q=tq_total // tq, past0=past0)
    return pl.pallas_call(
        kern,
        grid=(b, HEAD_PAIRS),
        in_specs=specs + [pl.BlockSpec((1, 1, 2, tk_total), lambda bi, hp: (bi, hp, 0, 0))],
        out_specs=ospec,
        out_shape=jax.ShapeDtypeStruct((b, tq_total, FOX_WIDTH), BF16),
        compiler_params=_cparams(("parallel", "parallel")),
        name="fox_attn",
    )(q, k_past, v_past, k_new, v_new, ck)


def _diff_attention(q, k_past, v_past, k_new, v_new, lam_row, g_row, out_scale):
    b, tq_total, tq, past0, specs, ospec = _attn_specs(q, k_past, k_new)
    kern = functools.partial(_diff_kernel, tq=tq, n_q=tq_total // tq, past0=past0, out_scale=out_scale)
    vec = pl.BlockSpec((1, LANES), lambda bi, hp: (0, 0))
    return pl.pallas_call(
        kern,
        grid=(b, HEAD_PAIRS),
        in_specs=specs + [vec, vec],
        out_specs=ospec,
        out_shape=jax.ShapeDtypeStruct((b, tq_total, DIFF_WIDTH), BF16),
        compiler_params=_cparams(("parallel", "parallel")),
        name="diff_attn",
    )(q, k_past, v_past, k_new, v_new, lam_row, g_row)


def _conv_kernel(buf_ref, w_ref, b_ref, g_ref, beta_ref, o_ref, *, t_total, tc):
    for c in range(t_total // tc):
        t0 = c * tc
        acc = jnp.zeros((tc, CONV_CH), F32)
        for w in range(CONV_WIDTH):
            acc = acc + buf_ref[0, t0 + w:t0 + w + tc, :] * w_ref[w:w + 1, :]
        y = acc + b_ref[...]
        mu = jnp.mean(y, axis=-1, keepdims=True)
        yc = y - mu
        var = jnp.mean(yc * yc, axis=-1, keepdims=True)
        yn = yc * lax.rsqrt(var + EPS) * g_ref[...] + beta_ref[...]
        o_ref[0, t0:t0 + tc, :] = (yn * jax.nn.sigmoid(yn)).astype(o_ref.dtype)


def _conv(buf, w, b, g, beta):
    bsz, tb, _ = buf.shape
    t_total = tb - (CONV_WIDTH - 1)
    tc = min(128, t_total)
    kern = functools.partial(_conv_kernel, t_total=t_total, tc=tc)
    full = lambda a: pl.BlockSpec(a.shape, lambda i: (0, 0))
    return pl.pallas_call(
        kern,
        grid=(bsz,),
        in_specs=[pl.BlockSpec((1, tb, CONV_CH), lambda i: (i, 0, 0)), full(w), full(b), full(g), full(beta)],
        out_specs=pl.BlockSpec((1, t_total, CONV_CH), lambda i: (i, 0, 0)),
        out_shape=jax.ShapeDtypeStruct((bsz, t_total, CONV_CH), BF16),
        compiler_params=_cparams(("parallel",)),
        name="conv",
    )(buf, w, b, g, beta)


def _outproj_kernel(x_ref, fo_ref, cy_ref, do_ref, w_ref, g2_ref, x1_o, hn_o):
    y = jnp.dot(fo_ref[...], w_ref[0:FOX_WIDTH, :], preferred_element_type=F32)
    y = y + jnp.dot(cy_ref[...], w_ref[FOX_WIDTH:FOX_WIDTH + CONV_CH, :], preferred_element_type=F32)
    y = y + jnp.dot(do_ref[...], w_ref[FOX_WIDTH + CONV_CH:, :], preferred_element_type=F32)
    x1 = x_ref[...] + y
    x1_o[...] = x1
    ms = jnp.mean(x1 * x1, axis=-1, keepdims=True)
    hn_o[...] = (x1 * lax.rsqrt(ms + EPS) * g2_ref[...]).astype(BF16)


def _outproj(x, fo, cy, do, w_out, g2):
    n = x.shape[0]
    tm = ROW_TILE
    row = lambda w: pl.BlockSpec((tm, w), lambda i: (i, 0))
    full = lambda a: pl.BlockSpec(a.shape, lambda i: (0, 0))
    return pl.pallas_call(
        _outproj_kernel,
        grid=(n // tm,),
        in_specs=[row(D_MODEL), row(FOX_WIDTH), row(CONV_CH), row(DIFF_WIDTH), full(w_out), full(g2)],
        out_specs=(row(D_MODEL), row(D_MODEL)),
        out_shape=(jax.ShapeDtypeStruct((n, D_MODEL), F32), jax.ShapeDtypeStruct((n, D_MODEL), BF16)),
        compiler_params=_cparams(("parallel",)),
        name="outproj",
    )(x, fo, cy, do, w_out, g2)


_STAIR = tuple(PEER_TOPK // (i + 1) for i in range(PEER_TOPK))


def _peer_route_kernel(hn_ref, wq_ref, k1_ref, k2_ref, cnt_o, a_o, r2_o, b_o, v1_scr, v2_scr):
    tt = hn_ref.shape[0]
    q = jnp.dot(hn_ref[...], wq_ref[...], preferred_element_type=F32).astype(BF16)
    sub = lax.broadcasted_iota(jnp.int32, (SUBLANES, tt), 0)
    for h in range(PEER_HEADS):
        q1 = q[:, h * PEER_QDIM:h * PEER_QDIM + N_KEYS]
        q2 = q[:, h * PEER_QDIM + N_KEYS:(h + 1) * PEER_QDIM]
        s1 = lax.dot_general(k1_ref[...], q1, _NT, preferred_element_type=F32)
        s2 = lax.dot_general(k2_ref[...], q2, _NT, preferred_element_type=F32)

        prev = None
        for r in range(PEER_TOPK):
            cur = jnp.max(s1 if prev is None else jnp.where(s1 < prev, s1, -jnp.inf), axis=0, keepdims=True)
            v1_scr[r:r + 1, :] = cur
            prev = cur
        prev = None
        rank2 = jnp.zeros_like(s2)
        for r in range(PEER_TOPK):
            if prev is None:
                cur = jnp.max(s2, axis=0, keepdims=True)
            else:
                below = s2 < prev
                rank2 = rank2 + jnp.where(below, 1.0, 0.0)
                cur = jnp.max(jnp.where(below, s2, -jnp.inf), axis=0, keepdims=True)
            v2_scr[r:r + 1, :] = cur
            prev = cur
        rank2 = rank2 + jnp.where(s2 < prev, 1.0, 0.0)

        v2lo = v2_scr[0:8, :]
        cands = [v1_scr[0:1, :] + v2lo, v1_scr[0:1, :] + v2_scr[8:16, :]]
        for i in range(1, 8):
            cands.append(jnp.where(sub < _STAIR[i], v1_scr[i:i + 1, :] + v2lo, -jnp.inf))
        cands.append(v1_scr[8:16, :] + v2_scr[0:1, :])
        prev = None
        for r in range(PEER_TOPK):
            best = None
            for c in cands:
                cm = c if prev is None else jnp.where(c < prev, c, -jnp.inf)
                best = cm if best is None else jnp.maximum(best, cm)
            prev = jnp.max(best, axis=0, keepdims=True)
        t16 = prev
        top = v1_scr[0:1, :] + v2_scr[0:1, :]
        z = None
        for c in cands:
            e = jnp.sum(jnp.where(c >= t16, jnp.exp(c - top), 0.0), axis=0, keepdims=True)
            z = e if z is None else z + e

        cnt = jnp.zeros_like(s1)
        for j in range(PEER_TOPK):
            cnt = jnp.where(s1 + v2_scr[j:j + 1, :] >= t16, float(j + 1), cnt)

        rows = slice(h * N_KEYS, (h + 1) * N_KEYS)
        cnt_o[rows, :] = cnt
        a_o[rows, :] = jnp.exp(s1 - v1_scr[0:1, :]) * (0.5 / z)
        r2_o[rows, :] = rank2
        b_o[rows, :] = jnp.exp(s2 - v2_scr[0:1, :])


def _peer_route(hn, wq, k1, k2):
    n = hn.shape[0]
    tt = ROW_TILE
    rows = PEER_HEADS * N_KEYS
    full = lambda a: pl.BlockSpec(a.shape, lambda i: (0, 0))
    col = pl.BlockSpec((rows, tt), lambda i: (0, i))
    return pl.pallas_call(
        _peer_route_kernel,
        grid=(n // tt,),
        in_specs=[pl.BlockSpec((tt, D_MODEL), lambda i: (i, 0)), full(wq), full(k1), full(k2)],
        out_specs=(col, col, col, col),
        out_shape=(jax.ShapeDtypeStruct((rows, n), F32), jax.ShapeDtypeStruct((rows, n), F32),
                   jax.ShapeDtypeStruct((rows, n), F32), jax.ShapeDtypeStruct((rows, n), F32)),
        scratch_shapes=[pltpu.VMEM((PEER_TOPK, tt), F32), pltpu.VMEM((PEER_TOPK, tt), F32)],
        compiler_params=_cparams(("parallel",)),
        name="peer_route",
    )(hn, wq, k1, k2)


_GELU_C = math.sqrt(2.0 / math.pi)


def _peer_dense_kernel(hn_ref, x1_ref, cnt_ref, a_ref, r2_ref, b_ref, u_ref, vt_ref, o_ref,
                       acc_scr, p_scr, r2_scr, b_scr):
    e = pl.program_id(1)
    tt = hn_ref.shape[0]
    n_tg = tt // LANES
    halves = 2
    hrows = EXPERT_BLOCK // halves
    keys_per_half = hrows // N_KEYS
    n_pk = N_KEYS // PACK
    zero_pk = jnp.zeros((PACK, LANES), BF16)

    @pl.when(e == 0)
    def _():
        acc_scr[...] = jnp.zeros_like(acc_scr)
        r2_scr[...] = r2_ref[...].astype(BF16)
        b_scr[...] = b_ref[...].astype(BF16)

    hn = hn_ref[...]
    base = pl.multiple_of(e * SUBLANES, SUBLANES)

    total = None
    for hf in range(halves):
        ht = lax.dot_general(u_ref[hf * hrows:(hf + 1) * hrows, :], hn, _NT, preferred_element_type=F32)
        for il in range(keys_per_half):
            i = hf * keys_per_half + il
            for tg in range(n_tg):
                cs = slice(tg * LANES, (tg + 1) * LANES)
                w = [None] * n_pk
                for h in range(PEER_HEADS):
                    cnt8 = cnt_ref[pl.ds(h * N_KEYS + base, SUBLANES), cs]
                    a8 = a_ref[pl.ds(h * N_KEYS + base, SUBLANES), cs]
                    crow = jnp.broadcast_to(cnt8[i:i + 1, :], (PACK, LANES)).astype(BF16)
                    arow = jnp.broadcast_to(a8[i:i + 1, :], (PACK, LANES)).astype(BF16)
                    for k in range(n_pk):
                        ks = slice(h * N_KEYS + k * PACK, h * N_KEYS + (k + 1) * PACK)
                        term = jnp.where(r2_scr[ks, cs] < crow, b_scr[ks, cs], zero_pk) * arow
                        w[k] = term if w[k] is None else w[k] + term
                for k in range(n_pk):
                    x = ht[il * N_KEYS + k * PACK:il * N_KEYS + (k + 1) * PACK, cs]
                    t = jnp.tanh(x * (_GELU_C + (_GELU_C * 0.044715) * (x * x)))
                    g = (x + x * t).astype(BF16)
                    p_scr[i * N_KEYS + k * PACK:i * N_KEYS + (k + 1) * PACK, cs] = w[k] * g
        part = jnp.dot(vt_ref[:, hf * hrows:(hf + 1) * hrows], p_scr[hf * hrows:(hf + 1) * hrows, :],
                       preferred_element_type=F32)
        total = part if total is None else total + part
    acc_scr[...] += total

    @pl.when(e == pl.num_programs(1) - 1)
    def _():
        o_ref[...] = x1_ref[...] + acc_scr[...].T


def _peer_dense(hn, x1, cnt, a, r2, b, u_bf, vt_bf):
    n = hn.shape[0]
    tt = ROW_TILE
    eb = EXPERT_BLOCK
    rows = PEER_HEADS * N_KEYS
    col = pl.BlockSpec((rows, tt), lambda i, e: (0, i))
    return pl.pallas_call(
        _peer_dense_kernel,
        grid=(n // tt, N_EXPERTS // eb),
        in_specs=[
            pl.BlockSpec((tt, D_MODEL), lambda i, e: (i, 0)),
            pl.BlockSpec((tt, D_MODEL), lambda i, e: (i, 0)),
            col, col, col, col,
            pl.BlockSpec((eb, D_MODEL), lambda i, e: (e, 0)),
            pl.BlockSpec((D_MODEL, eb), lambda i, e: (0, e)),
        ],
        out_specs=pl.BlockSpec((tt, D_MODEL), lambda i, e: (i, 0)),
        out_shape=jax.ShapeDtypeStruct((n, D_MODEL), F32),
        scratch_shapes=[pltpu.VMEM((D_MODEL, tt), F32), pltpu.VMEM((eb, tt), BF16),
                        pltpu.VMEM((rows, tt), BF16), pltpu.VMEM((rows, tt), BF16)],
        compiler_params=_cparams(("parallel", "arbitrary")),
        name="peer_dense",
    )(hn, x1, cnt, a, r2, b, u_bf, vt_bf)


def _pack_w_in(w):
    fq, fk, fv = w[:, 0:384], w[:, 384:768], w[:, 768:1152]
    ff = w[:, 1152:1158]
    ga, gb = w[:, 1158:1414], w[:, 1414:1670]
    dq, dk, dv = w[:, 1670:2054], w[:, 2054:2438], w[:, 2438:2822]
    ffp = jnp.pad(ff, ((0, 0), (0, LANES - FOX_HEADS)))
    return jnp.concatenate([fq, fk, fv, ga, gb, dq, dk, dv, ffp], axis=1).astype(BF16)


def _block_ones(width, group):
    idx = np.arange(width) // group
    return jnp.asarray((idx[:, None] == idx[None, :]).astype(np.float32), dtype=BF16)


def _rope_tables(positions):
    half = DIFF_QK_DIM // 2
    inv = ROPE_THETA ** (-jnp.arange(half, dtype=F32) / half)
    ang = positions.astype(F32)[:, None] * inv
    cos = jnp.cos(ang)
    sin = jnp.sin(ang)
    reps = DIFF_WIDTH // DIFF_QK_DIM
    cos_t = jnp.tile(jnp.concatenate([cos, cos], axis=1), (1, reps))
    sin_t = jnp.tile(jnp.concatenate([-sin, sin], axis=1), (1, reps))
    return cos_t, sin_t


def _layer_group(x, hist, tabs, wts, lam_init, bsz, t):
    (g1, w_pad, bf_pad, gq, gk, dgq, dgk, ones64, ones32, lam_row, gsub_row, conv_args, w_out, g2,
     wq, k1, k2, u_bf, vt_bf) = wts
    (fq, fk, fkb, fv, fvb, lf, u, dq, dk, dkb, dv, dvb) = _inproj(
        x, g1, w_pad, bf_pad, gq, gk, dgq, dgk, tabs[0], tabs[1], ones64, ones32)
    r3 = lambda a: a.reshape(bsz, t, -1)
    lf6 = r3(lf[:, :FOX_HEADS])
    u3 = r3(u)
    if hist is None:
        c_all = jnp.cumsum(lf6, axis=1)
        kp, vp, dkp, dvp = r3(fkb), r3(fvb), r3(dkb), r3(dvb)
        kn, vn, dkn, dvn = kp, vp, dkp, dvp
        buf = jnp.concatenate([jnp.zeros((bsz, CONV_WIDTH - 1, CONV_CH), F32), u3], axis=1)
    else:
        pk, pv, plf, pdk, pdv, conv_hist = hist
        past = pk.shape[1]
        c_all = jnp.cumsum(jnp.concatenate([plf.astype(F32), lf6], axis=1), axis=1)
        kp, vp = pk.reshape(bsz, past, FOX_WIDTH), pv.reshape(bsz, past, FOX_WIDTH)
        dkp, dvp = pdk.reshape(bsz, past, DIFF_WIDTH), pdv.reshape(bsz, past, DIFF_WIDTH)
        kn, vn, dkn, dvn = r3(fkb), r3(fvb), r3(dkb), r3(dvb)
        buf = jnp.concatenate([conv_hist.astype(F32), u3], axis=1)
    fox = _fox_attention(r3(fq), kp, vp, kn, vn, c_all)
    diff = _diff_attention(r3(dq), dkp, dvp, dkn, dvn, lam_row, gsub_row, 1.0 - lam_init)
    cy = _conv(buf, *conv_args)
    flat = lambda a: a.reshape(bsz * t, -1)
    x1, hn = _outproj(x, flat(fox), flat(cy), flat(diff), w_out, g2)
    cnt, a, r2, b = _peer_route(hn, wq, k1, k2)
    x_new = _peer_dense(hn, x1, cnt, a, r2, b, u_bf, vt_bf)
    hd = lambda arr, nh: arr.reshape(bsz, t, nh, -1)
    state = (hd(fk, FOX_HEADS), hd(fv, FOX_HEADS), lf6, hd(dk, DIFF_HEADS), hd(dv, DIFF_HEADS),
             buf[:, -(CONV_WIDTH - 1):])
    return x_new, state


def kernel(x_prompt, x_sample, cache_fox_k, cache_fox_v, cache_fox_logf, cache_diff_k, cache_diff_v, state_conv, norm1_g, w_in, b_forget, fox_q_g, fox_k_g, diff_q_g, diff_k_g, lam_q1, lam_k1, lam_q2, lam_k2, diff_sub_g, conv_w, conv_b, conv_ln_g, conv_ln_b, w_out, norm2_g, peer_w_q, peer_k1, peer_k2, peer_u, peer_v):
    bp, tp, d = x_prompt.shape
    bs, ts, _ = x_sample.shape
    depth = w_in.shape[0]
    past = cache_fox_k.shape[2]
    assert d == D_MODEL and tp % ROW_TILE == 0 and (bs * ts) % ROW_TILE == 0 and ROW_TILE % ts == 0

    xp = x_prompt.reshape(bp * tp, d)
    xs = x_sample.reshape(bs * ts, d)
    tabs_p = _rope_tables(jnp.arange(tp, dtype=jnp.int32))
    tabs_s = _rope_tables(jnp.tile(past + jnp.arange(ts, dtype=jnp.int32), ROW_TILE // ts))
    ones64 = _block_ones(FOX_WIDTH, HEAD_DIM)
    ones32 = _block_ones(DIFF_WIDTH, DIFF_QK_DIM)
    row = lambda v, reps: jnp.tile(v.astype(F32), reps)[None, :]

    p_st = [[] for _ in range(6)]
    s_st = [[] for _ in range(6)]
    for l in range(depth):
        lam_init = 0.8 - 0.6 * math.exp(-0.3 * l)
        lam = (jnp.exp(jnp.sum(lam_q1[l].astype(F32) * lam_k1[l].astype(F32)))
               - jnp.exp(jnp.sum(lam_q2[l].astype(F32) * lam_k2[l].astype(F32))) + lam_init)
        wts = (norm1_g[l][None, :], _pack_w_in(w_in[l]),
               jnp.pad(b_forget[l].astype(F32), (0, LANES - FOX_HEADS))[None, :],
               row(fox_q_g[l], FOX_HEADS), row(fox_k_g[l], FOX_HEADS),
               row(diff_q_g[l], 2 * DIFF_HEADS), row(diff_k_g[l], 2 * DIFF_HEADS), ones64, ones32,
               jnp.full((1, LANES), lam, F32), row(diff_sub_g[l], 2),
               (conv_w[l].astype(F32), conv_b[l][None, :], conv_ln_g[l][None, :], conv_ln_b[l][None, :]),
               w_out[l].astype(BF16), norm2_g[l][None, :],
               peer_w_q[l].astype(BF16), peer_k1[l].astype(BF16), peer_k2[l].astype(BF16),
               peer_u[l].astype(BF16), peer_v[l].T.astype(BF16))
        xp, stp = _layer_group(xp, None, tabs_p, wts, lam_init, bp, tp)
        hist = (cache_fox_k[l], cache_fox_v[l], cache_fox_logf[l], cache_diff_k[l], cache_diff_v[l],
                state_conv[l])
        xs, sts = _layer_group(xs, hist, tabs_s, wts, lam_init, bs, ts)
        for i in range(6):
            p_st[i].append(stp[i])
            s_st[i].append(sts[i])

    y_p = xp.reshape(bp, tp, d)
    y_s = xs.reshape(bs, ts, d)
    return (y_p, y_s) + tuple(jnp.stack(a) for a in p_st) + tuple(jnp.stack(a) for a in s_st)
```

```python
import functools
import math

import numpy as np
import jax
import jax.numpy as jnp
from jax import lax
from jax.experimental import pallas as pl
from jax.experimental.pallas import tpu as pltpu

F32 = jnp.float32
BF16 = jnp.bfloat16

D_MODEL = 1024
CHUNK = 64
HEAD_DIM = 64
FOX_HEADS = 6
FOX_WIDTH = FOX_HEADS * HEAD_DIM
CONV_CH = 256
CONV_WIDTH = 31
DIFF_HEADS = 6
DIFF_QK_DIM = 32
DIFF_V_DIM = 64
DIFF_WIDTH = DIFF_HEADS * DIFF_V_DIM
ROPE_THETA = 10000.0
PEER_HEADS = 8
PEER_QDIM = 256
N_KEYS = 128
N_EXPERTS = N_KEYS * N_KEYS
PEER_TOPK = 16
EPS = 1e-6
NEG = -1e30

LANES = 128
SUBLANES = 8
PACK = 16
HEAD_PAIRS = FOX_WIDTH // LANES
ROW_TILE = 512
ATTN_Q_BLOCK = 256
EXPERT_BLOCK = SUBLANES * N_KEYS
VMEM_LIMIT = 56 * 1024 * 1024

_SEG_FQ, _SEG_FK, _SEG_FV = 0, 384, 768
_SEG_GA, _SEG_GB = 1152, 1408
_SEG_DQ, _SEG_DK, _SEG_DV = 1664, 2048, 2432
_SEG_FF = 2816
IN_COLS_PAD = 2944

_NT = (((1,), (1,)), ((), ()))


def _cparams(sem, flags=None):
    return pltpu.CompilerParams(dimension_semantics=sem, vmem_limit_bytes=VMEM_LIMIT, flags=flags)


def _group_mean_sq(y, ones_ref, inv_n):
    y2 = y * y
    hi = y2.astype(BF16)
    lo = (y2 - hi.astype(F32)).astype(BF16)
    s = jnp.dot(hi, ones_ref[...], preferred_element_type=F32)
    s = s + jnp.dot(lo, ones_ref[...], preferred_element_type=F32)
    return s * inv_n


def _rope(y, cos, sin_signed):
    outs = []
    for c in range(y.shape[1] // LANES):
        sl = slice(c * LANES, (c + 1) * LANES)
        yc = y[:, sl]
        fwd = pltpu.roll(yc, LANES - DIFF_QK_DIM // 2, 1)
        bwd = pltpu.roll(yc, DIFF_QK_DIM // 2, 1)
        lane = lax.broadcasted_iota(jnp.int32, yc.shape, 1)
        partner = jnp.where((lane % DIFF_QK_DIM) < DIFF_QK_DIM // 2, fwd, bwd)
        outs.append(yc * cos[:, sl] + partner * sin_signed[:, sl])
    return jnp.concatenate(outs, axis=1)


def _inproj_kernel(x_ref, g1_ref, w_ref, bf_ref, gq_ref, gk_ref, dgq_ref, dgk_ref, cos_ref, sin_ref,
                   ones64_ref, ones32_ref,
                   fq_o, fk_o, fkb_o, fv_o, fvb_o, lf_o, u_o, dq_o, dk_o, dkb_o, dv_o, dvb_o):
    x = x_ref[...]
    ms = jnp.mean(x * x, axis=-1, keepdims=True)
    h = (x * lax.rsqrt(ms + EPS) * g1_ref[...]).astype(BF16)

    def seg(lo, width):
        return jnp.dot(h, w_ref[:, lo:lo + width], preferred_element_type=F32)

    fq = seg(_SEG_FQ, FOX_WIDTH)
    fq = fq * lax.rsqrt(_group_mean_sq(fq, ones64_ref, 1.0 / HEAD_DIM) + EPS) * gq_ref[...]
    fq_o[...] = (fq * (HEAD_DIM ** -0.5)).astype(BF16)

    fk = seg(_SEG_FK, FOX_WIDTH)
    fk = fk * lax.rsqrt(_group_mean_sq(fk, ones64_ref, 1.0 / HEAD_DIM) + EPS) * gk_ref[...]
    fk_o[...] = fk
    fkb_o[...] = fk.astype(BF16)

    fv = seg(_SEG_FV, FOX_WIDTH)
    fv_o[...] = fv
    fvb_o[...] = fv.astype(BF16)

    z = seg(_SEG_FF, LANES) + bf_ref[...]
    lf_o[...] = jnp.minimum(z, 0.0) - jnp.log1p(jnp.exp(-jnp.abs(z)))

    ga = seg(_SEG_GA, CONV_CH)
    gb = seg(_SEG_GB, CONV_CH)
    u_o[...] = ga * jax.nn.sigmoid(gb)

    cos = cos_ref[...]
    sin = sin_ref[...]
    dq = seg(_SEG_DQ, DIFF_WIDTH)
    dq = dq * lax.rsqrt(_group_mean_sq(dq, ones32_ref, 1.0 / DIFF_QK_DIM) + EPS) * dgq_ref[...]
    dq_o[...] = (_rope(dq, cos, sin) * (DIFF_QK_DIM ** -0.5)).astype(BF16)

    dk = seg(_SEG_DK, DIFF_WIDTH)
    dk = dk * lax.rsqrt(_group_mean_sq(dk, ones32_ref, 1.0 / DIFF_QK_DIM) + EPS) * dgk_ref[...]
    dk = _rope(dk, cos, sin)
    dk_o[...] = dk
    dkb_o[...] = dk.astype(BF16)

    dv = seg(_SEG_DV, DIFF_WIDTH)
    dv_o[...] = dv
    dvb_o[...] = dv.astype(BF16)


def _inproj(x, g1, w_pad, bf_pad, gq, gk, dgq, dgk, cos_tab, sin_tab, ones64, ones32):
    n = x.shape[0]
    tm = ROW_TILE
    pos_tiles = cos_tab.shape[0] // tm
    row = lambda w: pl.BlockSpec((tm, w), lambda i: (i, 0))
    full = lambda a: pl.BlockSpec(a.shape, lambda i: (0, 0))
    tab = pl.BlockSpec((tm, DIFF_WIDTH), lambda i: (i % pos_tiles, 0))
    w3 = FOX_WIDTH
    out_shape = (
        jax.ShapeDtypeStruct((n, w3), BF16),
        jax.ShapeDtypeStruct((n, w3), F32),
        jax.ShapeDtypeStruct((n, w3), BF16),
        jax.ShapeDtypeStruct((n, w3), F32),
        jax.ShapeDtypeStruct((n, w3), BF16),
        jax.ShapeDtypeStruct((n, LANES), F32),
        jax.ShapeDtypeStruct((n, CONV_CH), F32),
        jax.ShapeDtypeStruct((n, w3), BF16),
        jax.ShapeDtypeStruct((n, w3), F32),
        jax.ShapeDtypeStruct((n, w3), BF16),
        jax.ShapeDtypeStruct((n, w3), F32),
        jax.ShapeDtypeStruct((n, w3), BF16),
    )
    out_specs = (row(w3), row(w3), row(w3), row(w3), row(w3), row(LANES), row(CONV_CH),
                 row(w3), row(w3), row(w3), row(w3), row(w3))
    return pl.pallas_call(
        _inproj_kernel,
        grid=(n // tm,),
        in_specs=[row(D_MODEL), full(g1), full(w_pad), full(bf_pad), full(gq), full(gk), full(dgq),
                  full(dgk), tab, tab, full(ones64), full(ones32)],
        out_specs=out_specs,
        out_shape=out_shape,
        compiler_params=_cparams(("parallel",)),
        name="inproj",
    )(x, g1, w_pad, bf_pad, gq, gk, dgq, dgk, cos_tab, sin_tab, ones64, ones32)


def _softmax_pv(s_groups, v_parts, tq):
    p_groups, l_groups = [], []
    for s_parts in s_groups:
        m = None
        for s in s_parts:
            mi = jnp.max(s, axis=-1, keepdims=True)
            m = mi if m is None else jnp.maximum(m, mi)
        ps, l = [], None
        for s in s_parts:
            p = jnp.exp(s - m)
            li = jnp.sum(p, axis=-1, keepdims=True)
            l = li if l is None else l + li
            ps.append(p.astype(BF16))
        p_groups.append(ps)
        l_groups.append(l)
    acc = None
    for part, v in enumerate(v_parts):
        p_all = jnp.concatenate([ps[part] for ps in p_groups], axis=0)
        ai = jnp.dot(p_all, v, preferred_element_type=F32)
        acc = ai if acc is None else acc + ai
    return [acc[g * tq:(g + 1) * tq] / l for g, l in enumerate(l_groups)]


def _fox_kernel(q_ref, kp_ref, vp_ref, kn_ref, vn_ref, ck_ref, o_ref, *, tq, n_q, past0):
    lane = lax.broadcasted_iota(jnp.int32, (tq, LANES), 1)
    first = lane < HEAD_DIM
    causal = (lax.broadcasted_iota(jnp.int32, (tq, tq), 1) <= lax.broadcasted_iota(jnp.int32, (tq, tq), 0))
    for qi in range(n_q):
        past = past0 + qi * tq
        rows = slice(qi * tq, (qi + 1) * tq)
        q = q_ref[0, rows, :]
        kd = kn_ref[0, rows, :].astype(BF16)
        vd = vn_ref[0, rows, :].astype(BF16)
        if past:
            kp = kp_ref[0, 0:past, :].astype(BF16)
            vp = vp_ref[0, 0:past, :].astype(BF16)
        zq = jnp.zeros_like(q)
        q2 = jnp.concatenate([jnp.where(first, q, zq), jnp.where(first, zq, q)], axis=0)
        s_groups = [[], []]
        v_parts = []
        if past:
            s = lax.dot_general(q2, kp, _NT, preferred_element_type=F32)
            for j in range(2):
                s_groups[j].append(s[j * tq:(j + 1) * tq] - ck_ref[0, 0, j:j + 1, 0:past])
            v_parts.append(vp)
        s = lax.dot_general(q2, kd, _NT, preferred_element_type=F32)
        for j in range(2):
            sj = s[j * tq:(j + 1) * tq] - ck_ref[0, 0, j:j + 1, past:past + tq]
            s_groups[j].append(jnp.where(causal, sj, NEG))
        v_parts.append(vd)
        outs = _softmax_pv(s_groups, v_parts, tq)
        o_ref[0, rows, :] = jnp.where(first, outs[0], outs[1]).astype(o_ref.dtype)


def _diff_kernel(q_ref, kp_ref, vp_ref, kn_ref, vn_ref, lam_ref, g_ref, o_ref, *, tq, n_q, past0, out_scale):
    lane = lax.broadcasted_iota(jnp.int32, (tq, LANES), 1)
    first = lane < DIFF_V_DIM
    chunk_ok = (lax.broadcasted_iota(jnp.int32, (tq, tq), 1) // CHUNK
                <= lax.broadcasted_iota(jnp.int32, (tq, tq), 0) // CHUNK)
    lam = lam_ref[...]
    for qi in range(n_q):
        past = past0 + qi * tq
        rows = slice(qi * tq, (qi + 1) * tq)
        q = q_ref[0, rows, :]
        kd = kn_ref[0, rows, :].astype(BF16)
        vd = vn_ref[0, rows, :].astype(BF16)
        if past:
            kp = kp_ref[0, 0:past, :].astype(BF16)
            vp = vp_ref[0, 0:past, :].astype(BF16)
        zq = jnp.zeros_like(q)
        qms = []
        for g in range(4):
            lo = g * DIFF_QK_DIM
            sel = jnp.where(lane >= lo, lane, LANES) < lo + DIFF_QK_DIM
            qms.append(jnp.where(sel, q, zq))
        q4 = jnp.concatenate(qms, axis=0)
        s_groups = [[] for _ in range(4)]
        v_parts = []
        if past:
            s = lax.dot_general(q4, kp, _NT, preferred_element_type=F32)
            for g in range(4):
                s_groups[g].append(s[g * tq:(g + 1) * tq])
            v_parts.append(vp)
        s = lax.dot_general(q4, kd, _NT, preferred_element_type=F32)
        for g in range(4):
            s_groups[g].append(jnp.where(chunk_ok, s[g * tq:(g + 1) * tq], NEG))
        v_parts.append(vd)
        maps = _softmax_pv(s_groups, v_parts, tq)
        outs = [maps[0] - lam * maps[1], maps[2] - lam * maps[3]]
        o = jnp.where(first, outs[0], outs[1])
        o2 = o * o
        ss0 = jnp.sum(jnp.where(first, o2, 0.0), axis=-1, keepdims=True)
        ss1 = jnp.sum(jnp.where(first, 0.0, o2), axis=-1, keepdims=True)
        ms = jnp.where(first, ss0, ss1) * (1.0 / DIFF_V_DIM)
        o = o * lax.rsqrt(ms + EPS) * g_ref[...]
        o_ref[0, rows, :] = (o * out_scale).astype(o_ref.dtype)


def _attn_specs(q, k_past, k_new):
    b, tq_total, _ = q.shape
    tq = min(ATTN_Q_BLOCK, tq_total)
    same = k_past is k_new
    past0 = 0 if same else k_past.shape[1]
    blk = lambda t: pl.BlockSpec((1, t, LANES), lambda bi, hp: (bi, 0, hp))
    specs = [blk(tq_total), blk(k_past.shape[1]), blk(k_past.shape[1]), blk(tq_total), blk(tq_total)]
    return b, tq_total, tq, past0, specs, blk(tq_total)


def _fox_attention(q, k_past, v_past, k_new, v_new, c_all):
    b, tq_total, tq, past0, specs, ospec = _attn_specs(q, k_past, k_new)
    tk_total = c_all.shape[1]
    ck = c_all.transpose(0, 2, 1).reshape(b, HEAD_PAIRS, 2, tk_total)
    kern = functools.partial(_fox_kernel, tq=tq, n_q=tq_total // tq, past0=past0)
    return pl.pallas_call(
        kern,
        grid=(b, HEAD_PAIRS),
        in_specs=specs + [pl.BlockSpec((1, 1, 2, tk_total), lambda bi, hp: (bi, hp, 0, 0))],
        out_specs=ospec,
        out_shape=jax.ShapeDtypeStruct((b, tq_total, FOX_WIDTH), BF16),
        compiler_params=_cparams(("parallel", "parallel")),
        name="fox_attn",
    )(q, k_past, v_past, k_new, v_new, ck)


def _diff_attention(q, k_past, v_past, k_new, v_new, lam_row, g_row, out_scale):
    b, tq_total, tq, past0, specs, ospec = _attn_specs(q, k_past, k_new)
    kern = functools.partial(_diff_kernel, tq=tq, n_q=tq_total // tq, past0=past0, out_scale=out_scale)
    vec = pl.BlockSpec((1, LANES), lambda bi, hp: (0, 0))
    return pl.pallas_call(
        kern,
        grid=(b, HEAD_PAIRS),
        in_specs=specs + [vec, vec],
        out_specs=ospec,
        out_shape=jax.ShapeDtypeStruct((b, tq_total, DIFF_WIDTH), BF16),
        compiler_params=_cparams(("parallel", "parallel")),
        name="diff_attn",
    )(q, k_past, v_past, k_new, v_new, lam_row, g_row)


def _conv_kernel(hist_ref, u_ref, w_ref, b_ref, g_ref, beta_ref, o_ref, win_scr, *, t_total, tc):
    pad = _CONV_ALIGN - (CONV_WIDTH - 1)
    win_scr[pad:_CONV_ALIGN, :] = hist_ref[0]
    win_scr[_CONV_ALIGN:_CONV_ALIGN + t_total, :] = u_ref[0]
    for c in range(t_total // tc):
        t0 = c * tc
        acc = jnp.zeros((tc, CONV_CH), F32)
        for w in range(CONV_WIDTH):
            acc = acc + win_scr[pad + t0 + w:pad + t0 + w + tc, :] * w_ref[w:w + 1, :]
        y = acc + b_ref[...]
        mu = jnp.mean(y, axis=-1, keepdims=True)
        yc = y - mu
        var = jnp.mean(yc * yc, axis=-1, keepdims=True)
        yn = yc * lax.rsqrt(var + EPS) * g_ref[...] + beta_ref[...]
        o_ref[0, t0:t0 + tc, :] = (yn * jax.nn.sigmoid(yn)).astype(o_ref.dtype)


_CONV_ALIGN = 32


def _conv(hist, u, w, b, g, beta):
    bsz, t_total, _ = u.shape
    tc = min(128, t_total)
    kern = functools.partial(_conv_kernel, t_total=t_total, tc=tc)
    full = lambda a: pl.BlockSpec(a.shape, lambda i: (0, 0))
    return pl.pallas_call(
        kern,
        grid=(bsz,),
        in_specs=[pl.BlockSpec((1, CONV_WIDTH - 1, CONV_CH), lambda i: (i, 0, 0)),
                  pl.BlockSpec((1, t_total, CONV_CH), lambda i: (i, 0, 0)), full(w), full(b), full(g), full(beta)],
        out_specs=pl.BlockSpec((1, t_total, CONV_CH), lambda i: (i, 0, 0)),
        out_shape=jax.ShapeDtypeStruct((bsz, t_total, CONV_CH), BF16),
        scratch_shapes=[pltpu.VMEM((_CONV_ALIGN + t_total, CONV_CH), F32)],
        compiler_params=_cparams(("parallel",)),
        name="conv",
    )(hist, u, w, b, g, beta)


def _outproj_kernel(x_ref, fo_ref, cy_ref, do_ref, w_ref, g2_ref, x1_o, hn_o, hnt_o):
    y = jnp.dot(fo_ref[...], w_ref[0:FOX_WIDTH, :], preferred_element_type=F32)
    y = y + jnp.dot(cy_ref[...], w_ref[FOX_WIDTH:FOX_WIDTH + CONV_CH, :], preferred_element_type=F32)
    y = y + jnp.dot(do_ref[...], w_ref[FOX_WIDTH + CONV_CH:, :], preferred_element_type=F32)
    x1 = x_ref[...] + y
    x1_o[...] = x1
    ms = jnp.mean(x1 * x1, axis=-1, keepdims=True)
    hn = x1 * lax.rsqrt(ms + EPS) * g2_ref[...]
    hn_o[...] = hn.astype(BF16)
    hnt_o[...] = hn.T.astype(BF16)


def _outproj(x, fo, cy, do, w_out, g2):
    n = x.shape[0]
    tm = ROW_TILE
    row = lambda w: pl.BlockSpec((tm, w), lambda i: (i, 0))
    full = lambda a: pl.BlockSpec(a.shape, lambda i: (0, 0))
    return pl.pallas_call(
        _outproj_kernel,
        grid=(n // tm,),
        in_specs=[row(D_MODEL), row(FOX_WIDTH), row(CONV_CH), row(DIFF_WIDTH), full(w_out), full(g2)],
        out_specs=(row(D_MODEL), row(D_MODEL), pl.BlockSpec((D_MODEL, tm), lambda i: (0, i))),
        out_shape=(jax.ShapeDtypeStruct((n, D_MODEL), F32), jax.ShapeDtypeStruct((n, D_MODEL), BF16),
                   jax.ShapeDtypeStruct((D_MODEL, n), BF16)),
        compiler_params=_cparams(("parallel",)),
        name="outproj",
    )(x, fo, cy, do, w_out, g2)


_STAIR = tuple(PEER_TOPK // (i + 1) for i in range(PEER_TOPK))


def _dup_bf16(x):
    bits = pltpu.bitcast(x.astype(BF16).astype(F32), jnp.uint32)
    return bits | (bits >> 16)


def _peer_route_kernel(hn_ref, wq_ref, k1_ref, k2_ref, cnt_o, a_o, r2_o, b_o, v1_scr, v2_scr):
    tt = hn_ref.shape[0]
    q = jnp.dot(hn_ref[...], wq_ref[...], preferred_element_type=F32).astype(BF16)
    sub = lax.broadcasted_iota(jnp.int32, (SUBLANES, tt), 0)
    for h in range(PEER_HEADS):
        q1 = q[:, h * PEER_QDIM:h * PEER_QDIM + N_KEYS]
        q2 = q[:, h * PEER_QDIM + N_KEYS:(h + 1) * PEER_QDIM]
        s1 = lax.dot_general(k1_ref[...], q1, _NT, preferred_element_type=F32)
        s2 = lax.dot_general(k2_ref[...], q2, _NT, preferred_element_type=F32)

        prev = None
        for r in range(PEER_TOPK):
            cur = jnp.max(s1 if prev is None else jnp.where(s1 < prev, s1, -jnp.inf), axis=0, keepdims=True)
            v1_scr[r:r + 1, :] = cur
            prev = cur
        prev = None
        rank2 = jnp.zeros_like(s2)
        for r in range(PEER_TOPK):
            if prev is None:
                cur = jnp.max(s2, axis=0, keepdims=True)
            else:
                below = s2 < prev
                rank2 = rank2 + jnp.where(below, 1.0, 0.0)
                cur = jnp.max(jnp.where(below, s2, -jnp.inf), axis=0, keepdims=True)
            v2_scr[r:r + 1, :] = cur
            prev = cur
        rank2 = rank2 + jnp.where(s2 < prev, 1.0, 0.0)

        v2lo = v2_scr[0:8, :]
        cands = [v1_scr[0:1, :] + v2lo, v1_scr[0:1, :] + v2_scr[8:16, :]]
        for i in range(1, 8):
            cands.append(jnp.where(sub < _STAIR[i], v1_scr[i:i + 1, :] + v2lo, -jnp.inf))
        cands.append(v1_scr[8:16, :] + v2_scr[0:1, :])
        prev = None
        for r in range(PEER_TOPK):
            best = None
            for c in cands:
                cm = c if prev is None else jnp.where(c < prev, c, -jnp.inf)
                best = cm if best is None else jnp.maximum(best, cm)
            prev = jnp.max(best, axis=0, keepdims=True)
        t16 = prev
        top = v1_scr[0:1, :] + v2_scr[0:1, :]
        z = None
        for c in cands:
            e = jnp.sum(jnp.where(c >= t16, jnp.exp(c - top), 0.0), axis=0, keepdims=True)
            z = e if z is None else z + e

        cnt = jnp.zeros_like(s1)
        for j in range(PEER_TOPK):
            cnt = jnp.where(s1 + v2_scr[j:j + 1, :] >= t16, float(j + 1), cnt)

        rows = slice(h * N_KEYS, (h + 1) * N_KEYS)
        cnt_o[rows, :] = _dup_bf16(cnt)
        a_o[rows, :] = _dup_bf16(jnp.exp(s1 - v1_scr[0:1, :]) * (0.5 / z))
        r2_o[rows, :] = rank2
        b_o[rows, :] = jnp.exp(s2 - v2_scr[0:1, :])


def _peer_route(hn, wq, k1, k2):
    n = hn.shape[0]
    tt = ROW_TILE
    rows = PEER_HEADS * N_KEYS
    full = lambda a: pl.BlockSpec(a.shape, lambda i: (0, 0))
    col = pl.BlockSpec((rows, tt), lambda i: (0, i))
    return pl.pallas_call(
        _peer_route_kernel,
        grid=(n // tt,),
        in_specs=[pl.BlockSpec((tt, D_MODEL), lambda i: (i, 0)), full(wq), full(k1), full(k2)],
        out_specs=(col, col, col, col),
        out_shape=(jax.ShapeDtypeStruct((rows, n), jnp.uint32), jax.ShapeDtypeStruct((rows, n), jnp.uint32),
                   jax.ShapeDtypeStruct((rows, n), F32), jax.ShapeDtypeStruct((rows, n), F32)),
        scratch_shapes=[pltpu.VMEM((PEER_TOPK, tt), F32), pltpu.VMEM((PEER_TOPK, tt), F32)],
        compiler_params=_cparams(("parallel",)),
        name="peer_route",
    )(hn, wq, k1, k2)


_GELU_C = math.sqrt(2.0 / math.pi)


def _peer_dense_kernel(hnt_ref, x1_ref, cnt_ref, a_ref, r2_ref, b_ref, u_ref, vt_ref, o_ref,
                       acc_scr, p_scr, rb_scr, hnt_scr):
    e = pl.program_id(1)
    tt = hnt_ref.shape[1]
    n_tg = tt // LANES
    halves = 2
    hrows = EXPERT_BLOCK // halves
    keys_per_half = hrows // N_KEYS
    n_pk = N_KEYS // PACK
    zero_pk = jnp.zeros((PACK, LANES), BF16)
    n_chunks = PEER_HEADS * n_pk

    @pl.when(e == 0)
    def _():
        acc_scr[...] = jnp.zeros_like(acc_scr)
        hnt_scr[...] = hnt_ref[...]
        for tg in range(n_tg):
            cs = slice(tg * LANES, (tg + 1) * LANES)
            for c in range(n_chunks):
                src = slice(c * PACK, (c + 1) * PACK)
                rb_scr[tg, 2 * c * PACK:(2 * c + 1) * PACK, :] = r2_ref[src, cs].astype(BF16)
                rb_scr[tg, (2 * c + 1) * PACK:(2 * c + 2) * PACK, :] = b_ref[src, cs].astype(BF16)

    total = None
    for hf in range(halves):
        rows_h = slice(hf * hrows, (hf + 1) * hrows)
        ht = jnp.dot(u_ref[rows_h, :], hnt_scr[...], preferred_element_type=F32)
        for il in range(keys_per_half):
            i = hf * keys_per_half + il
            for tg in range(n_tg):
                cs = slice(tg * LANES, (tg + 1) * LANES)
                w = [None] * n_pk
                for h in range(PEER_HEADS):
                    cnt8 = cnt_ref[h, 0, :, cs]
                    a8 = a_ref[h, 0, :, cs]
                    crow = pltpu.bitcast(jnp.broadcast_to(cnt8[i:i + 1, :], (SUBLANES, LANES)), BF16)
                    arow = pltpu.bitcast(jnp.broadcast_to(a8[i:i + 1, :], (SUBLANES, LANES)), BF16)
                    for k in range(n_pk):
                        c = h * n_pk + k
                        r2c = rb_scr[tg, 2 * c * PACK:(2 * c + 1) * PACK, :]
                        bc = rb_scr[tg, (2 * c + 1) * PACK:(2 * c + 2) * PACK, :]
                        term = jnp.where(r2c < crow, bc, zero_pk) * arow
                        w[k] = term if w[k] is None else w[k] + term
                for k in range(n_pk):
                    rk = slice(i * N_KEYS + k * PACK, i * N_KEYS + (k + 1) * PACK)
                    x = ht[il * N_KEYS + k * PACK:il * N_KEYS + (k + 1) * PACK, cs]
                    t = jnp.tanh(x * (_GELU_C + (_GELU_C * 0.044715) * (x * x)))
                    g = (x + x * t).astype(BF16)
                    p_scr[tg, rk, :] = w[k] * g
        p_half = jnp.concatenate([p_scr[tg, rows_h, :] for tg in range(n_tg)], axis=1)
        part = jnp.dot(vt_ref[:, rows_h], p_half, preferred_element_type=F32)
        total = part if total is None else total + part
    acc_scr[...] += total

    @pl.when(e == pl.num_programs(1) - 1)
    def _():
        o_ref[...] = x1_ref[...] + acc_scr[...].T


def _transpose_cast_kernel(v_ref, o_ref):
    o_ref[...] = v_ref[...].T.astype(o_ref.dtype)


def _transpose_cast(v, dtype):
    r, c = v.shape
    tr = ROW_TILE
    return pl.pallas_call(
        _transpose_cast_kernel,
        grid=(r // tr,),
        in_specs=[pl.BlockSpec((tr, c), lambda i: (i, 0))],
        out_specs=pl.BlockSpec((c, tr), lambda i: (0, i)),
        out_shape=jax.ShapeDtypeStruct((c, r), dtype),
        compiler_params=_cparams(("parallel",)),
        name="transpose_cast",
    )(v)


def _peer_dense(hnt, x1, cnt, a, r2, b, u_bf, vt_bf):
    n = x1.shape[0]
    tt = ROW_TILE
    eb = EXPERT_BLOCK
    rows = PEER_HEADS * N_KEYS
    col = pl.BlockSpec((rows, tt), lambda i, e: (0, i))
    n_blocks = N_EXPERTS // eb
    cnt = cnt.reshape(PEER_HEADS, n_blocks, SUBLANES, n)
    a = a.reshape(PEER_HEADS, n_blocks, SUBLANES, n)
    key_rows = pl.BlockSpec((PEER_HEADS, 1, SUBLANES, tt), lambda i, e: (0, e, 0, i))
    return pl.pallas_call(
        _peer_dense_kernel,
        grid=(n // tt, n_blocks),
        in_specs=[
            pl.BlockSpec((D_MODEL, tt), lambda i, e: (0, i)),
            pl.BlockSpec((tt, D_MODEL), lambda i, e: (i, 0)),
            key_rows, key_rows, col, col,
            pl.BlockSpec((eb, D_MODEL), lambda i, e: (e, 0)),
            pl.BlockSpec((D_MODEL, eb), lambda i, e: (0, e)),
        ],
        out_specs=pl.BlockSpec((tt, D_MODEL), lambda i, e: (i, 0)),
        out_shape=jax.ShapeDtypeStruct((n, D_MODEL), F32),
        scratch_shapes=[pltpu.VMEM((D_MODEL, tt), F32), pltpu.VMEM((tt // LANES, eb, LANES), BF16),
                        pltpu.VMEM((tt // LANES, 2 * rows, LANES), BF16),
                        pltpu.VMEM((D_MODEL, tt), BF16)],
        compiler_params=_cparams(("parallel", "arbitrary")),
        name="peer_dense",
    )(hnt, x1, cnt, a, r2, b, u_bf, vt_bf)


def _pack_w_in(w):
    fq, fk, fv = w[:, 0:384], w[:, 384:768], w[:, 768:1152]
    ff = w[:, 1152:1158]
    ga, gb = w[:, 1158:1414], w[:, 1414:1670]
    dq, dk, dv = w[:, 1670:2054], w[:, 2054:2438], w[:, 2438:2822]
    ffp = jnp.pad(ff, ((0, 0), (0, LANES - FOX_HEADS)))
    return jnp.concatenate([fq, fk, fv, ga, gb, dq, dk, dv, ffp], axis=1).astype(BF16)


def _block_ones(width, group):
    idx = np.arange(width) // group
    return jnp.asarray((idx[:, None] == idx[None, :]).astype(np.float32), dtype=BF16)


def _rope_tables(positions):
    half = DIFF_QK_DIM // 2
    inv = ROPE_THETA ** (-jnp.arange(half, dtype=F32) / half)
    ang = positions.astype(F32)[:, None] * inv
    cos = jnp.cos(ang)
    sin = jnp.sin(ang)
    reps = DIFF_WIDTH // DIFF_QK_DIM
    cos_t = jnp.tile(jnp.concatenate([cos, cos], axis=1), (1, reps))
    sin_t = jnp.tile(jnp.concatenate([-sin, sin], axis=1), (1, reps))
    return cos_t, sin_t


def _layer_group(x, hist, tabs, wts, lam_init, bsz, t):
    (g1, w_pad, bf_pad, gq, gk, dgq, dgk, ones64, ones32, lam_row, gsub_row, conv_args, w_out, g2,
     wq, k1, k2, u_bf, vt_bf) = wts
    (fq, fk, fkb, fv, fvb, lf, u, dq, dk, dkb, dv, dvb) = _inproj(
        x, g1, w_pad, bf_pad, gq, gk, dgq, dgk, tabs[0], tabs[1], ones64, ones32)
    r3 = lambda a: a.reshape(bsz, t, -1)
    lf6 = r3(lf[:, :FOX_HEADS])
    u3 = r3(u)
    if hist is None:
        c_all = jnp.cumsum(lf6, axis=1)
        kp, vp, dkp, dvp = r3(fkb), r3(fvb), r3(dkb), r3(dvb)
        kn, vn, dkn, dvn = kp, vp, dkp, dvp
        conv_hist = jnp.zeros((bsz, CONV_WIDTH - 1, CONV_CH), F32)
    else:
        pk, pv, plf, pdk, pdv, conv_hist = hist
        past = pk.shape[1]
        c_all = jnp.cumsum(jnp.concatenate([plf.astype(F32), lf6], axis=1), axis=1)
        kp, vp = pk.reshape(bsz, past, FOX_WIDTH), pv.reshape(bsz, past, FOX_WIDTH)
        dkp, dvp = pdk.reshape(bsz, past, DIFF_WIDTH), pdv.reshape(bsz, past, DIFF_WIDTH)
        kn, vn, dkn, dvn = r3(fkb), r3(fvb), r3(dkb), r3(dvb)
        conv_hist = conv_hist.astype(F32)
    fox = _fox_attention(r3(fq), kp, vp, kn, vn, c_all)
    diff = _diff_attention(r3(dq), dkp, dvp, dkn, dvn, lam_row, gsub_row, 1.0 - lam_init)
    cy = _conv(conv_hist, u3, *conv_args)
    flat = lambda a: a.reshape(bsz * t, -1)
    x1, hn, hnt = _outproj(x, flat(fox), flat(cy), flat(diff), w_out, g2)
    cnt, a, r2, b = _peer_route(hn, wq, k1, k2)
    x_new = _peer_dense(hnt, x1, cnt, a, r2, b, u_bf, vt_bf)
    hd = lambda arr, nh: arr.reshape(bsz, t, nh, -1)
    keep = CONV_WIDTH - 1
    new_conv = u3[:, -keep:] if t >= keep else jnp.concatenate([conv_hist, u3], axis=1)[:, -keep:]
    state = (hd(fk, FOX_HEADS), hd(fv, FOX_HEADS), lf6, hd(dk, DIFF_HEADS), hd(dv, DIFF_HEADS), new_conv)
    return x_new, state


def kernel(x_prompt, x_sample, cache_fox_k, cache_fox_v, cache_fox_logf, cache_diff_k, cache_diff_v, state_conv, norm1_g, w_in, b_forget, fox_q_g, fox_k_g, diff_q_g, diff_k_g, lam_q1, lam_k1, lam_q2, lam_k2, diff_sub_g, conv_w, conv_b, conv_ln_g, conv_ln_b, w_out, norm2_g, peer_w_q, peer_k1, peer_k2, peer_u, peer_v):
    bp, tp, d = x_prompt.shape
    bs, ts, _ = x_sample.shape
    depth = w_in.shape[0]
    past = cache_fox_k.shape[2]
    assert d == D_MODEL and tp % ROW_TILE == 0 and (bs * ts) % ROW_TILE == 0 and ROW_TILE % ts == 0

    xp = x_prompt.reshape(bp * tp, d)
    xs = x_sample.reshape(bs * ts, d)
    tabs_p = _rope_tables(jnp.arange(tp, dtype=jnp.int32))
    tabs_s = _rope_tables(jnp.tile(past + jnp.arange(ts, dtype=jnp.int32), ROW_TILE // ts))
    ones64 = _block_ones(FOX_WIDTH, HEAD_DIM)
    ones32 = _block_ones(DIFF_WIDTH, DIFF_QK_DIM)
    row = lambda v, reps: jnp.tile(v.astype(F32), reps)[None, :]

    p_st = [[] for _ in range(6)]
    s_st = [[] for _ in range(6)]
    for l in range(depth):
        lam_init = 0.8 - 0.6 * math.exp(-0.3 * l)
        lam = (jnp.exp(jnp.sum(lam_q1[l].astype(F32) * lam_k1[l].astype(F32)))
               - jnp.exp(jnp.sum(lam_q2[l].astype(F32) * lam_k2[l].astype(F32))) + lam_init)
        wts = (norm1_g[l][None, :], _pack_w_in(w_in[l]),
               jnp.pad(b_forget[l].astype(F32), (0, LANES - FOX_HEADS))[None, :],
               row(fox_q_g[l], FOX_HEADS), row(fox_k_g[l], FOX_HEADS),
               row(diff_q_g[l], 2 * DIFF_HEADS), row(diff_k_g[l], 2 * DIFF_HEADS), ones64, ones32,
               jnp.full((1, LANES), lam, F32), row(diff_sub_g[l], 2),
               (conv_w[l].astype(F32), conv_b[l][None, :], conv_ln_g[l][None, :], conv_ln_b[l][None, :]),
               w_out[l].astype(BF16), norm2_g[l][None, :],
               peer_w_q[l].astype(BF16), peer_k1[l].astype(BF16), peer_k2[l].astype(BF16),
               peer_u[l].astype(BF16), _transpose_cast(peer_v[l], BF16))
        xp, stp = _layer_group(xp, None, tabs_p, wts, lam_init, bp, tp)
        hist = (cache_fox_k[l], cache_fox_v[l], cache_fox_logf[l], cache_diff_k[l], cache_diff_v[l],
                state_conv[l])
        xs, sts = _layer_group(xs, hist, tabs_s, wts, lam_init, bs, ts)
        for i in range(6):
            p_st[i].append(stp[i])
            s_st[i].append(sts[i])

    y_p = xp.reshape(bp, tp, d)
    y_s = xs.reshape(bs, ts, d)
    return (y_p, y_s) + tuple(jnp.stack(a) for a in p_st) + tuple(jnp.stack(a) for a in s_st)
```

```python
import functools
import math

import numpy as np
import jax
import jax.numpy as jnp
from jax import lax
from jax.experimental import pallas as pl
from jax.experimental.pallas import tpu as pltpu

F32 = jnp.float32
BF16 = jnp.bfloat16

D_MODEL = 1024
CHUNK = 64
HEAD_DIM = 64
FOX_HEADS = 6
FOX_WIDTH = FOX_HEADS * HEAD_DIM
CONV_CH = 256
CONV_WIDTH = 31
DIFF_HEADS = 6
DIFF_QK_DIM = 32
DIFF_V_DIM = 64
DIFF_WIDTH = DIFF_HEADS * DIFF_V_DIM
ROPE_THETA = 10000.0
PEER_HEADS = 8
PEER_QDIM = 256
N_KEYS = 128
N_EXPERTS = N_KEYS * N_KEYS
PEER_TOPK = 16
EPS = 1e-6
NEG = -1e30

LANES = 128
SUBLANES = 8
PACK = 16
HEAD_PAIRS = FOX_WIDTH // LANES
ROW_TILE = 512
ATTN_Q_BLOCK = 256
EXPERT_BLOCK = SUBLANES * N_KEYS
VMEM_LIMIT = 56 * 1024 * 1024

_SEG_FQ, _SEG_FK, _SEG_FV = (0, 0), (0, 384), (0, 768)
_SEG_GA, _SEG_GB = (1, 0), (1, 256)
_SEG_DQ, _SEG_DK, _SEG_DV = (1, 512), (1, 896), (1, 1280)
_SEG_FF = (2, 0)
_FF_LO, _FF_HI = 3 * FOX_WIDTH, 3 * FOX_WIDTH + FOX_HEADS

_NT = (((1,), (1,)), ((), ()))


def _cparams(sem, flags=None):
    return pltpu.CompilerParams(dimension_semantics=sem, vmem_limit_bytes=VMEM_LIMIT, flags=flags)


def _group_mean_sq(y, ones_ref, inv_n):
    y2 = y * y
    hi = y2.astype(BF16)
    lo = (y2 - hi.astype(F32)).astype(BF16)
    s = jnp.dot(hi, ones_ref[...], preferred_element_type=F32)
    s = s + jnp.dot(lo, ones_ref[...], preferred_element_type=F32)
    return s * inv_n


def _rope(y, cos, sin_signed):
    outs = []
    for c in range(y.shape[1] // LANES):
        sl = slice(c * LANES, (c + 1) * LANES)
        yc = y[:, sl]
        fwd = pltpu.roll(yc, LANES - DIFF_QK_DIM // 2, 1)
        bwd = pltpu.roll(yc, DIFF_QK_DIM // 2, 1)
        lane = lax.broadcasted_iota(jnp.int32, yc.shape, 1)
        partner = jnp.where((lane % DIFF_QK_DIM) < DIFF_QK_DIM // 2, fwd, bwd)
        outs.append(yc * cos[:, sl] + partner * sin_signed[:, sl])
    return jnp.concatenate(outs, axis=1)


def _inproj_kernel(x_ref, g1_ref, wa_ref, wb_ref, wf_ref, bf_ref, gq_ref, gk_ref, dgq_ref, dgk_ref, cos_ref, sin_ref,
                   ones64_ref, ones32_ref,
                   fq_o, fk_o, fkb_o, fv_o, fvb_o, lf_o, u_o, dq_o, dk_o, dkb_o, dv_o, dvb_o):
    x = x_ref[...]
    ms = jnp.mean(x * x, axis=-1, keepdims=True)
    h = (x * lax.rsqrt(ms + EPS) * g1_ref[...]).astype(BF16)

    w_refs = (wa_ref, wb_ref, wf_ref)

    def seg(where, width):
        piece, lo = where
        return jnp.dot(h, w_refs[piece][:, lo:lo + width], preferred_element_type=F32)

    fq = seg(_SEG_FQ, FOX_WIDTH)
    fq = fq * lax.rsqrt(_group_mean_sq(fq, ones64_ref, 1.0 / HEAD_DIM) + EPS) * gq_ref[...]
    fq_o[...] = (fq * (HEAD_DIM ** -0.5)).astype(BF16)

    fk = seg(_SEG_FK, FOX_WIDTH)
    fk = fk * lax.rsqrt(_group_mean_sq(fk, ones64_ref, 1.0 / HEAD_DIM) + EPS) * gk_ref[...]
    fk_o[...] = fk
    fkb_o[...] = fk.astype(BF16)

    fv = seg(_SEG_FV, FOX_WIDTH)
    fv_o[...] = fv
    fvb_o[...] = fv.astype(BF16)

    z = seg(_SEG_FF, LANES) + bf_ref[...]
    lf_o[...] = jnp.minimum(z, 0.0) - jnp.log1p(jnp.exp(-jnp.abs(z)))

    ga = seg(_SEG_GA, CONV_CH)
    gb = seg(_SEG_GB, CONV_CH)
    u_o[...] = ga * jax.nn.sigmoid(gb)

    cos = cos_ref[...]
    sin = sin_ref[...]
    dq = seg(_SEG_DQ, DIFF_WIDTH)
    dq = dq * lax.rsqrt(_group_mean_sq(dq, ones32_ref, 1.0 / DIFF_QK_DIM) + EPS) * dgq_ref[...]
    dq_o[...] = (_rope(dq, cos, sin) * (DIFF_QK_DIM ** -0.5)).astype(BF16)

    dk = seg(_SEG_DK, DIFF_WIDTH)
    dk = dk * lax.rsqrt(_group_mean_sq(dk, ones32_ref, 1.0 / DIFF_QK_DIM) + EPS) * dgk_ref[...]
    dk = _rope(dk, cos, sin)
    dk_o[...] = dk
    dkb_o[...] = dk.astype(BF16)

    dv = seg(_SEG_DV, DIFF_WIDTH)
    dv_o[...] = dv
    dvb_o[...] = dv.astype(BF16)


def _inproj(x, g1, w_pad, bf_pad, gq, gk, dgq, dgk, cos_tab, sin_tab, ones64, ones32):
    n = x.shape[0]
    tm = ROW_TILE
    pos_tiles = cos_tab.shape[0] // tm
    row = lambda w: pl.BlockSpec((tm, w), lambda i: (i, 0))
    full = lambda a: pl.BlockSpec(a.shape, lambda i: (0, 0))
    tab = pl.BlockSpec((tm, DIFF_WIDTH), lambda i: (i % pos_tiles, 0))
    w3 = FOX_WIDTH
    out_shape = (
        jax.ShapeDtypeStruct((n, w3), BF16),
        jax.ShapeDtypeStruct((n, w3), F32),
        jax.ShapeDtypeStruct((n, w3), BF16),
        jax.ShapeDtypeStruct((n, w3), F32),
        jax.ShapeDtypeStruct((n, w3), BF16),
        jax.ShapeDtypeStruct((n, LANES), F32),
        jax.ShapeDtypeStruct((n, CONV_CH), F32),
        jax.ShapeDtypeStruct((n, w3), BF16),
        jax.ShapeDtypeStruct((n, w3), F32),
        jax.ShapeDtypeStruct((n, w3), BF16),
        jax.ShapeDtypeStruct((n, w3), F32),
        jax.ShapeDtypeStruct((n, w3), BF16),
    )
    out_specs = (row(w3), row(w3), row(w3), row(w3), row(w3), row(LANES), row(CONV_CH),
                 row(w3), row(w3), row(w3), row(w3), row(w3))
    return pl.pallas_call(
        _inproj_kernel,
        grid=(n // tm,),
        in_specs=[row(D_MODEL), full(g1), full(w_pad[0]), full(w_pad[1]), full(w_pad[2]), full(bf_pad),
                  full(gq), full(gk), full(dgq),
                  full(dgk), tab, tab, full(ones64), full(ones32)],
        out_specs=out_specs,
        out_shape=out_shape,
        compiler_params=_cparams(("parallel",)),
        name="inproj",
    )(x, g1, *w_pad, bf_pad, gq, gk, dgq, dgk, cos_tab, sin_tab, ones64, ones32)


def _softmax_pv(s_groups, v_parts, tq):
    p_groups, l_groups = [], []
    for s_parts in s_groups:
        m = None
        for s in s_parts:
            mi = jnp.max(s, axis=-1, keepdims=True)
            m = mi if m is None else jnp.maximum(m, mi)
        ps, l = [], None
        for s in s_parts:
            p = jnp.exp(s - m)
            li = jnp.sum(p, axis=-1, keepdims=True)
            l = li if l is None else l + li
            ps.append(p.astype(BF16))
        p_groups.append(ps)
        l_groups.append(l)
    acc = None
    for part, v in enumerate(v_parts):
        p_all = jnp.concatenate([ps[part] for ps in p_groups], axis=0)
        ai = jnp.dot(p_all, v, preferred_element_type=F32)
        acc = ai if acc is None else acc + ai
    return [acc[g * tq:(g + 1) * tq] / l for g, l in enumerate(l_groups)]


def _fox_kernel(q_ref, kp_ref, vp_ref, kn_ref, vn_ref, ck_ref, o_ref, *, tq, n_q, past0):
    lane = lax.broadcasted_iota(jnp.int32, (tq, LANES), 1)
    first = lane < HEAD_DIM
    causal = (lax.broadcasted_iota(jnp.int32, (tq, tq), 1) <= lax.broadcasted_iota(jnp.int32, (tq, tq), 0))
    for qi in range(n_q):
        past = past0 + qi * tq
        rows = slice(qi * tq, (qi + 1) * tq)
        q = q_ref[0, rows, :]
        kd = kn_ref[0, rows, :].astype(BF16)
        vd = vn_ref[0, rows, :].astype(BF16)
        if past:
            kp = kp_ref[0, 0:past, :].astype(BF16)
            vp = vp_ref[0, 0:past, :].astype(BF16)
        zq = jnp.zeros_like(q)
        q2 = jnp.concatenate([jnp.where(first, q, zq), jnp.where(first, zq, q)], axis=0)
        s_groups = [[], []]
        v_parts = []
        if past:
            s = lax.dot_general(q2, kp, _NT, preferred_element_type=F32)
            for j in range(2):
                s_groups[j].append(s[j * tq:(j + 1) * tq] - ck_ref[0, 0, j:j + 1, 0:past])
            v_parts.append(vp)
        s = lax.dot_general(q2, kd, _NT, preferred_element_type=F32)
        for j in range(2):
            sj = s[j * tq:(j + 1) * tq] - ck_ref[0, 0, j:j + 1, past:past + tq]
            s_groups[j].append(jnp.where(causal, sj, NEG))
        v_parts.append(vd)
        outs = _softmax_pv(s_groups, v_parts, tq)
        o_ref[0, rows, :] = jnp.where(first, outs[0], outs[1]).astype(o_ref.dtype)


def _diff_kernel(q_ref, kp_ref, vp_ref, kn_ref, vn_ref, lam_ref, g_ref, o_ref, *, tq, n_q, past0, out_scale):
    lane = lax.broadcasted_iota(jnp.int32, (tq, LANES), 1)
    first = lane < DIFF_V_DIM
    chunk_ok = (lax.broadcasted_iota(jnp.int32, (tq, tq), 1) // CHUNK
                <= lax.broadcasted_iota(jnp.int32, (tq, tq), 0) // CHUNK)
    lam = lam_ref[...]
    for qi in range(n_q):
        past = past0 + qi * tq
        rows = slice(qi * tq, (qi + 1) * tq)
        q = q_ref[0, rows, :]
        kd = kn_ref[0, rows, :].astype(BF16)
        vd = vn_ref[0, rows, :].astype(BF16)
        if past:
            kp = kp_ref[0, 0:past, :].astype(BF16)
            vp = vp_ref[0, 0:past, :].astype(BF16)
        zq = jnp.zeros_like(q)
        qms = []
        for g in range(4):
            lo = g * DIFF_QK_DIM
            sel = jnp.where(lane >= lo, lane, LANES) < lo + DIFF_QK_DIM
            qms.append(jnp.where(sel, q, zq))
        q4 = jnp.concatenate(qms, axis=0)
        s_groups = [[] for _ in range(4)]
        v_parts = []
        if past:
            s = lax.dot_general(q4, kp, _NT, preferred_element_type=F32)
            for g in range(4):
                s_groups[g].append(s[g * tq:(g + 1) * tq])
            v_parts.append(vp)
        s = lax.dot_general(q4, kd, _NT, preferred_element_type=F32)
        for g in range(4):
            s_groups[g].append(jnp.where(chunk_ok, s[g * tq:(g + 1) * tq], NEG))
        v_parts.append(vd)
        maps = _softmax_pv(s_groups, v_parts, tq)
        outs = [maps[0] - lam * maps[1], maps[2] - lam * maps[3]]
        o = jnp.where(first, outs[0], outs[1])
        o2 = o * o
        ss0 = jnp.sum(jnp.where(first, o2, 0.0), axis=-1, keepdims=True)
        ss1 = jnp.sum(jnp.where(first, 0.0, o2), axis=-1, keepdims=True)
        ms = jnp.where(first, ss0, ss1) * (1.0 / DIFF_V_DIM)
        o = o * lax.rsqrt(ms + EPS) * g_ref[...]
        o_ref[0, rows, :] = (o * out_scale).astype(o_ref.dtype)


def _cumsum_kernel(lf_ref, tri_ref, o_ref, *, t_total):
    carry = jnp.zeros((SUBLANES, 1), F32)
    tri = tri_ref[...]
    for blk in range(t_total // LANES):
        cols = slice(blk * LANES, (blk + 1) * LANES)
        x8 = lf_ref[0, cols, :].T[0:SUBLANES, :]
        hi = x8.astype(BF16)
        r1 = x8 - hi.astype(F32)
        mid = r1.astype(BF16)
        lo = (r1 - mid.astype(F32)).astype(BF16)
        c = (jnp.dot(hi, tri, preferred_element_type=F32) + jnp.dot(mid, tri, preferred_element_type=F32)
             + jnp.dot(lo, tri, preferred_element_type=F32)) + carry
        o_ref[0, :, cols] = c
        carry = c[:, LANES - 1:LANES]


def _cumsum_time(lf):
    pad = -lf.shape[1] % LANES
    if pad:
        lf = jnp.pad(lf, ((0, 0), (0, pad), (0, 0)))
    b, t_total, _ = lf.shape
    idx = np.arange(LANES)
    tri = jnp.asarray((idx[:, None] <= idx[None, :]).astype(np.float32), dtype=BF16)
    kern = functools.partial(_cumsum_kernel, t_total=t_total)
    return pl.pallas_call(
        kern,
        grid=(b,),
        in_specs=[pl.BlockSpec((1, t_total, LANES), lambda i: (i, 0, 0)),
                  pl.BlockSpec((LANES, LANES), lambda i: (0, 0))],
        out_specs=pl.BlockSpec((1, SUBLANES, t_total), lambda i: (i, 0, 0)),
        out_shape=jax.ShapeDtypeStruct((b, SUBLANES, t_total), F32),
        compiler_params=_cparams(("parallel",)),
        name="cumsum_time",
    )(lf, tri)


def _attn_specs(q, k_past, k_new):
    b, tq_total, _ = q.shape
    tq = min(ATTN_Q_BLOCK, tq_total)
    same = k_past is k_new
    past0 = 0 if same else k_past.shape[1]
    blk = lambda t: pl.BlockSpec((1, t, LANES), lambda bi, hp: (bi, 0, hp))
    specs = [blk(tq_total), blk(k_past.shape[1]), blk(k_past.shape[1]), blk(tq_total), blk(tq_total)]
    return b, tq_total, tq, past0, specs, blk(tq_total)


def _fox_attention(q, k_past, v_past, k_new, v_new, c_all):
    b, tq_total, tq, past0, specs, ospec = _attn_specs(q, k_past, k_new)
    tk_total = c_all.shape[2]
    ck = c_all.reshape(b, SUBLANES // 2, 2, tk_total)
    kern = functools.partial(_fox_kernel, tq=tq, n_q=tq_total // tq, past0=past0)
    return pl.pallas_call(
        kern,
        grid=(b, HEAD_PAIRS),
        in_specs=specs + [pl.BlockSpec((1, 1, 2, tk_total), lambda bi, hp: (bi, hp, 0, 0))],
        out_specs=ospec,
        out_shape=jax.ShapeDtypeStruct((b, tq_total, FOX_WIDTH), BF16),
        compiler_params=_cparams(("parallel", "parallel")),
        name="fox_attn",
    )(q, k_past, v_past, k_new, v_new, ck)


def _diff_attention(q, k_past, v_past, k_new, v_new, lam_row, g_row, out_scale):
    b, tq_total, tq, past0, specs, ospec = _attn_specs(q, k_past, k_new)
    kern = functools.partial(_diff_kernel, tq=tq, n_q=tq_total // tq, past0=past0, out_scale=out_scale)
    vec = pl.BlockSpec((1, LANES), lambda bi, hp: (0, 0))
    return pl.pallas_call(
        kern,
        grid=(b, HEAD_PAIRS),
        in_specs=specs + [vec, vec],
        out_specs=ospec,
        out_shape=jax.ShapeDtypeStruct((b, tq_total, DIFF_WIDTH), BF16),
        compiler_params=_cparams(("parallel", "parallel")),
        name="diff_attn",
    )(q, k_past, v_past, k_new, v_new, lam_row, g_row)


def _conv_kernel(hist_ref, u_ref, w_ref, b_ref, g_ref, beta_ref, o_ref, win_scr, *, t_total, tc):
    pad = _CONV_ALIGN - (CONV_WIDTH - 1)
    win_scr[pad:_CONV_ALIGN, :] = hist_ref[0]
    win_scr[_CONV_ALIGN:_CONV_ALIGN + t_total, :] = u_ref[0]
    for c in range(t_total // tc):
        t0 = c * tc
        acc = jnp.zeros((tc, CONV_CH), F32)
        for w in range(CONV_WIDTH):
            acc = acc + win_scr[pad + t0 + w:pad + t0 + w + tc, :] * w_ref[w:w + 1, :]
        y = acc + b_ref[...]
        mu = jnp.mean(y, axis=-1, keepdims=True)
        yc = y - mu
        var = jnp.mean(yc * yc, axis=-1, keepdims=True)
        yn = yc * lax.rsqrt(var + EPS) * g_ref[...] + beta_ref[...]
        o_ref[0, t0:t0 + tc, :] = (yn * jax.nn.sigmoid(yn)).astype(o_ref.dtype)


_CONV_ALIGN = 32


def _conv(hist, u, w, b, g, beta):
    bsz, t_total, _ = u.shape
    tc = min(128, t_total)
    kern = functools.partial(_conv_kernel, t_total=t_total, tc=tc)
    full = lambda a: pl.BlockSpec(a.shape, lambda i: (0, 0))
    return pl.pallas_call(
        kern,
        grid=(bsz,),
        in_specs=[pl.BlockSpec((1, CONV_WIDTH - 1, CONV_CH), lambda i: (i, 0, 0)),
                  pl.BlockSpec((1, t_total, CONV_CH), lambda i: (i, 0, 0)), full(w), full(b), full(g), full(beta)],
        out_specs=pl.BlockSpec((1, t_total, CONV_CH), lambda i: (i, 0, 0)),
        out_shape=jax.ShapeDtypeStruct((bsz, t_total, CONV_CH), BF16),
        scratch_shapes=[pltpu.VMEM((_CONV_ALIGN + t_total, CONV_CH), F32)],
        compiler_params=_cparams(("parallel",)),
        name="conv",
    )(hist, u, w, b, g, beta)


def _outproj_kernel(x_ref, fo_ref, cy_ref, do_ref, w_ref, g2_ref, x1_o, hn_o, hnt_o):
    y = jnp.dot(fo_ref[...], w_ref[0:FOX_WIDTH, :], preferred_element_type=F32)
    y = y + jnp.dot(cy_ref[...], w_ref[FOX_WIDTH:FOX_WIDTH + CONV_CH, :], preferred_element_type=F32)
    y = y + jnp.dot(do_ref[...], w_ref[FOX_WIDTH + CONV_CH:, :], preferred_element_type=F32)
    x1 = x_ref[...] + y
    x1_o[...] = x1
    ms = jnp.mean(x1 * x1, axis=-1, keepdims=True)
    hn = x1 * lax.rsqrt(ms + EPS) * g2_ref[...]
    hn_o[...] = hn.astype(BF16)
    hnt_o[...] = hn.T.astype(BF16)


def _outproj(x, fo, cy, do, w_out, g2):
    n = x.shape[0]
    tm = ROW_TILE
    row = lambda w: pl.BlockSpec((tm, w), lambda i: (i, 0))
    full = lambda a: pl.BlockSpec(a.shape, lambda i: (0, 0))
    return pl.pallas_call(
        _outproj_kernel,
        grid=(n // tm,),
        in_specs=[row(D_MODEL), row(FOX_WIDTH), row(CONV_CH), row(DIFF_WIDTH), full(w_out), full(g2)],
        out_specs=(row(D_MODEL), row(D_MODEL), pl.BlockSpec((D_MODEL, tm), lambda i: (0, i))),
        out_shape=(jax.ShapeDtypeStruct((n, D_MODEL), F32), jax.ShapeDtypeStruct((n, D_MODEL), BF16),
                   jax.ShapeDtypeStruct((D_MODEL, n), BF16)),
        compiler_params=_cparams(("parallel",)),
        name="outproj",
    )(x, fo, cy, do, w_out, g2)


_STAIR = tuple(PEER_TOPK // (i + 1) for i in range(PEER_TOPK))


def _dup_bf16(x):
    bits = pltpu.bitcast(x.astype(BF16).astype(F32), jnp.uint32)
    return bits | (bits >> 16)


def _peer_route_kernel(hn_ref, wq_ref, k1_ref, k2_ref, cnt_o, a_o, r2_o, b_o, v1_scr, v2_scr):
    tt = hn_ref.shape[0]
    q = jnp.dot(hn_ref[...], wq_ref[...], preferred_element_type=F32).astype(BF16)
    sub = lax.broadcasted_iota(jnp.int32, (SUBLANES, tt), 0)
    for h in range(PEER_HEADS):
        q1 = q[:, h * PEER_QDIM:h * PEER_QDIM + N_KEYS]
        q2 = q[:, h * PEER_QDIM + N_KEYS:(h + 1) * PEER_QDIM]
        s1 = lax.dot_general(k1_ref[...], q1, _NT, preferred_element_type=F32)
        s2 = lax.dot_general(k2_ref[...], q2, _NT, preferred_element_type=F32)

        prev = None
        for r in range(PEER_TOPK):
            cur = jnp.max(s1 if prev is None else jnp.where(s1 < prev, s1, -jnp.inf), axis=0, keepdims=True)
            v1_scr[r:r + 1, :] = cur
            prev = cur
        prev = None
        rank2 = jnp.zeros_like(s2)
        for r in range(PEER_TOPK):
            if prev is None:
                cur = jnp.max(s2, axis=0, keepdims=True)
            else:
                below = s2 < prev
                rank2 = rank2 + jnp.where(below, 1.0, 0.0)
                cur = jnp.max(jnp.where(below, s2, -jnp.inf), axis=0, keepdims=True)
            v2_scr[r:r + 1, :] = cur
            prev = cur
        rank2 = rank2 + jnp.where(s2 < prev, 1.0, 0.0)

        v2lo = v2_scr[0:8, :]
        cands = [v1_scr[0:1, :] + v2lo, v1_scr[0:1, :] + v2_scr[8:16, :]]
        for i in range(1, 8):
            cands.append(jnp.where(sub < _STAIR[i], v1_scr[i:i + 1, :] + v2lo, -jnp.inf))
        cands.append(v1_scr[8:16, :] + v2_scr[0:1, :])
        prev = None
        for r in range(PEER_TOPK):
            best = None
            for c in cands:
                cm = c if prev is None else jnp.where(c < prev, c, -jnp.inf)
                best = cm if best is None else jnp.maximum(best, cm)
            prev = jnp.max(best, axis=0, keepdims=True)
        t16 = prev
        top = v1_scr[0:1, :] + v2_scr[0:1, :]
        z = None
        for c in cands:
            e = jnp.sum(jnp.where(c >= t16, jnp.exp(c - top), 0.0), axis=0, keepdims=True)
            z = e if z is None else z + e

        cnt = jnp.zeros_like(s1)
        for j in range(PEER_TOPK):
            cnt = jnp.where(s1 + v2_scr[j:j + 1, :] >= t16, float(j + 1), cnt)

        rows = slice(h * N_KEYS, (h + 1) * N_KEYS)
        cnt_o[rows, :] = _dup_bf16(cnt)
        a_o[rows, :] = _dup_bf16(jnp.exp(s1 - v1_scr[0:1, :]) * (0.5 / z))
        r2_o[rows, :] = rank2
        b_o[rows, :] = jnp.exp(s2 - v2_scr[0:1, :])


def _peer_route(hn, wq, k1, k2):
    n = hn.shape[0]
    tt = ROW_TILE
    rows = PEER_HEADS * N_KEYS
    full = lambda a: pl.BlockSpec(a.shape, lambda i: (0, 0))
    col = pl.BlockSpec((rows, tt), lambda i: (0, i))
    return pl.pallas_call(
        _peer_route_kernel,
        grid=(n // tt,),
        in_specs=[pl.BlockSpec((tt, D_MODEL), lambda i: (i, 0)), full(wq), full(k1), full(k2)],
        out_specs=(col, col, col, col),
        out_shape=(jax.ShapeDtypeStruct((rows, n), jnp.uint32), jax.ShapeDtypeStruct((rows, n), jnp.uint32),
                   jax.ShapeDtypeStruct((rows, n), F32), jax.ShapeDtypeStruct((rows, n), F32)),
        scratch_shapes=[pltpu.VMEM((PEER_TOPK, tt), F32), pltpu.VMEM((PEER_TOPK, tt), F32)],
        compiler_params=_cparams(("parallel",)),
        name="peer_route",
    )(hn, wq, k1, k2)


_GELU_C = math.sqrt(2.0 / math.pi)


def _peer_dense_kernel(hnt_ref, x1_ref, cnt_ref, a_ref, r2_ref, b_ref, u_ref, vt_ref, o_ref,
                       acc_scr, p_scr, rb_scr, hnt_scr):
    e = pl.program_id(1)
    tt = hnt_ref.shape[1]
    n_tg = tt // LANES
    halves = 2
    hrows = EXPERT_BLOCK // halves
    keys_per_half = hrows // N_KEYS
    n_pk = N_KEYS // PACK
    zero_pk = jnp.zeros((PACK, LANES), BF16)
    n_chunks = PEER_HEADS * n_pk

    @pl.when(e == 0)
    def _():
        acc_scr[...] = jnp.zeros_like(acc_scr)
        hnt_scr[...] = hnt_ref[...]
        for tg in range(n_tg):
            cs = slice(tg * LANES, (tg + 1) * LANES)
            for c in range(n_chunks):
                src = slice(c * PACK, (c + 1) * PACK)
                rb_scr[tg, 2 * c * PACK:(2 * c + 1) * PACK, :] = r2_ref[src, cs].astype(BF16)
                rb_scr[tg, (2 * c + 1) * PACK:(2 * c + 2) * PACK, :] = b_ref[src, cs].astype(BF16)

    total = None
    for hf in range(halves):
        rows_h = slice(hf * hrows, (hf + 1) * hrows)
        ht = jnp.dot(u_ref[rows_h, :], hnt_scr[...], preferred_element_type=F32)
        for il in range(keys_per_half):
            i = hf * keys_per_half + il
            for tg in range(n_tg):
                cs = slice(tg * LANES, (tg + 1) * LANES)
                w = [None] * n_pk
                for h in range(PEER_HEADS):
                    cnt8 = cnt_ref[h, 0, :, cs]
                    a8 = a_ref[h, 0, :, cs]
                    crow = pltpu.bitcast(jnp.broadcast_to(cnt8[i:i + 1, :], (SUBLANES, LANES)), BF16)
                    arow = pltpu.bitcast(jnp.broadcast_to(a8[i:i + 1, :], (SUBLANES, LANES)), BF16)
                    for k in range(n_pk):
                        c = h * n_pk + k
                        r2c = rb_scr[tg, 2 * c * PACK:(2 * c + 1) * PACK, :]
                        bc = rb_scr[tg, (2 * c + 1) * PACK:(2 * c + 2) * PACK, :]
                        term = jnp.where(r2c < crow, bc, zero_pk) * arow
                        w[k] = term if w[k] is None else w[k] + term
                for k in range(n_pk):
                    rk = slice(i * N_KEYS + k * PACK, i * N_KEYS + (k + 1) * PACK)
                    x = ht[il * N_KEYS + k * PACK:il * N_KEYS + (k + 1) * PACK, cs]
                    t = jnp.tanh(x * (_GELU_C + (_GELU_C * 0.044715) * (x * x)))
                    g = (x + x * t).astype(BF16)
                    p_scr[tg, rk, :] = w[k] * g
        p_half = jnp.concatenate([p_scr[tg, rows_h, :] for tg in range(n_tg)], axis=1)
        part = jnp.dot(vt_ref[:, rows_h], p_half, preferred_element_type=F32)
        total = part if total is None else total + part
    acc_scr[...] += total

    @pl.when(e == pl.num_programs(1) - 1)
    def _():
        o_ref[...] = x1_ref[...] + acc_scr[...].T


def _transpose_cast_kernel(v_ref, o_ref):
    o_ref[...] = v_ref[0].T.astype(o_ref.dtype)


def _transpose_cast(v, layer, dtype):
    _, r, c = v.shape
    tr = ROW_TILE
    return pl.pallas_call(
        _transpose_cast_kernel,
        grid=(r // tr,),
        in_specs=[pl.BlockSpec((1, tr, c), lambda i: (layer, i, 0))],
        out_specs=pl.BlockSpec((c, tr), lambda i: (0, i)),
        out_shape=jax.ShapeDtypeStruct((c, r), dtype),
        compiler_params=_cparams(("parallel",)),
        name="transpose_cast",
    )(v)


def _peer_dense(hnt, x1, cnt, a, r2, b, u_bf, vt_bf):
    n = x1.shape[0]
    tt = ROW_TILE
    eb = EXPERT_BLOCK
    rows = PEER_HEADS * N_KEYS
    col = pl.BlockSpec((rows, tt), lambda i, e: (0, i))
    n_blocks = N_EXPERTS // eb
    cnt = cnt.reshape(PEER_HEADS, n_blocks, SUBLANES, n)
    a = a.reshape(PEER_HEADS, n_blocks, SUBLANES, n)
    key_rows = pl.BlockSpec((PEER_HEADS, 1, SUBLANES, tt), lambda i, e: (0, e, 0, i))
    return pl.pallas_call(
        _peer_dense_kernel,
        grid=(n // tt, n_blocks),
        in_specs=[
            pl.BlockSpec((D_MODEL, tt), lambda i, e: (0, i)),
            pl.BlockSpec((tt, D_MODEL), lambda i, e: (i, 0)),
            key_rows, key_rows, col, col,
            pl.BlockSpec((eb, D_MODEL), lambda i, e: (e, 0)),
            pl.BlockSpec((D_MODEL, eb), lambda i, e: (0, e)),
        ],
        out_specs=pl.BlockSpec((tt, D_MODEL), lambda i, e: (i, 0)),
        out_shape=jax.ShapeDtypeStruct((n, D_MODEL), F32),
        scratch_shapes=[pltpu.VMEM((D_MODEL, tt), F32), pltpu.VMEM((tt // LANES, eb, LANES), BF16),
                        pltpu.VMEM((tt // LANES, 2 * rows, LANES), BF16),
                        pltpu.VMEM((D_MODEL, tt), BF16)],
        compiler_params=_cparams(("parallel", "arbitrary")),
        name="peer_dense",
    )(hnt, x1, cnt, a, r2, b, u_bf, vt_bf)


def _pack_w_in(w):
    ffp = jnp.pad(w[:, _FF_LO:_FF_HI], ((0, 0), (0, LANES - FOX_HEADS)))
    return w[:, :_FF_LO].astype(BF16), w[:, _FF_HI:].astype(BF16), ffp.astype(BF16)


def _block_ones(width, group):
    idx = np.arange(width) // group
    return jnp.asarray((idx[:, None] == idx[None, :]).astype(np.float32), dtype=BF16)


def _rope_tables(positions):
    half = DIFF_QK_DIM // 2
    inv = ROPE_THETA ** (-jnp.arange(half, dtype=F32) / half)
    ang = positions.astype(F32)[:, None] * inv
    cos = jnp.cos(ang)
    sin = jnp.sin(ang)
    reps = DIFF_WIDTH // DIFF_QK_DIM
    cos_t = jnp.tile(jnp.concatenate([cos, cos], axis=1), (1, reps))
    sin_t = jnp.tile(jnp.concatenate([-sin, sin], axis=1), (1, reps))
    return cos_t, sin_t


def _layer_group(x, hist, tabs, wts, lam_init, bsz, t):
    (g1, w_pad, bf_pad, gq, gk, dgq, dgk, ones64, ones32, lam_row, gsub_row, conv_args, w_out, g2,
     wq, k1, k2, u_bf, vt_bf) = wts
    (fq, fk, fkb, fv, fvb, lf, u, dq, dk, dkb, dv, dvb) = _inproj(
        x, g1, w_pad, bf_pad, gq, gk, dgq, dgk, tabs[0], tabs[1], ones64, ones32)
    r3 = lambda a: a.reshape(bsz, t, -1)
    lf3 = r3(lf)
    lf6 = lf3[:, :, :FOX_HEADS]
    u3 = r3(u)
    if hist is None:
        c_all = _cumsum_time(lf3)
        kp, vp, dkp, dvp = r3(fkb), r3(fvb), r3(dkb), r3(dvb)
        kn, vn, dkn, dvn = kp, vp, dkp, dvp
        conv_hist = jnp.zeros((bsz, CONV_WIDTH - 1, CONV_CH), F32)
    else:
        pk, pv, plf, pdk, pdv, conv_hist = hist
        past = pk.shape[1]
        plf_pad = jnp.pad(plf.astype(F32), ((0, 0), (0, 0), (0, LANES - FOX_HEADS)))
        c_all = _cumsum_time(jnp.concatenate([plf_pad, lf3], axis=1))
        kp, vp = pk.reshape(bsz, past, FOX_WIDTH), pv.reshape(bsz, past, FOX_WIDTH)
        dkp, dvp = pdk.reshape(bsz, past, DIFF_WIDTH), pdv.reshape(bsz, past, DIFF_WIDTH)
        kn, vn, dkn, dvn = r3(fkb), r3(fvb), r3(dkb), r3(dvb)
        conv_hist = conv_hist.astype(F32)
    fox = _fox_attention(r3(fq), kp, vp, kn, vn, c_all)
    diff = _diff_attention(r3(dq), dkp, dvp, dkn, dvn, lam_row, gsub_row, 1.0 - lam_init)
    cy = _conv(conv_hist, u3, *conv_args)
    flat = lambda a: a.reshape(bsz * t, -1)
    x1, hn, hnt = _outproj(x, flat(fox), flat(cy), flat(diff), w_out, g2)
    cnt, a, r2, b = _peer_route(hn, wq, k1, k2)
    x_new = _peer_dense(hnt, x1, cnt, a, r2, b, u_bf, vt_bf)
    hd = lambda arr, nh: arr.reshape(bsz, t, nh, -1)
    keep = CONV_WIDTH - 1
    new_conv = u3[:, -keep:] if t >= keep else jnp.concatenate([conv_hist, u3], axis=1)[:, -keep:]
    state = (hd(fk, FOX_HEADS), hd(fv, FOX_HEADS), lf6, hd(dk, DIFF_HEADS), hd(dv, DIFF_HEADS), new_conv)
    return x_new, state


def kernel(x_prompt, x_sample, cache_fox_k, cache_fox_v, cache_fox_logf, cache_diff_k, cache_diff_v, state_conv, norm1_g, w_in, b_forget, fox_q_g, fox_k_g, diff_q_g, diff_k_g, lam_q1, lam_k1, lam_q2, lam_k2, diff_sub_g, conv_w, conv_b, conv_ln_g, conv_ln_b, w_out, norm2_g, peer_w_q, peer_k1, peer_k2, peer_u, peer_v):
    bp, tp, d = x_prompt.shape
    bs, ts, _ = x_sample.shape
    depth = w_in.shape[0]
    past = cache_fox_k.shape[2]
    assert d == D_MODEL and tp % ROW_TILE == 0 and (bs * ts) % ROW_TILE == 0 and ROW_TILE % ts == 0

    xp = x_prompt.reshape(bp * tp, d)
    xs = x_sample.reshape(bs * ts, d)
    tabs_p = _rope_tables(jnp.arange(tp, dtype=jnp.int32))
    tabs_s = _rope_tables(jnp.tile(past + jnp.arange(ts, dtype=jnp.int32), ROW_TILE // ts))
    ones64 = _block_ones(FOX_WIDTH, HEAD_DIM)
    ones32 = _block_ones(DIFF_WIDTH, DIFF_QK_DIM)
    row = lambda v, reps: jnp.tile(v.astype(F32), reps)[None, :]

    p_st = [[] for _ in range(6)]
    s_st = [[] for _ in range(6)]
    for l in range(depth):
        lam_init = 0.8 - 0.6 * math.exp(-0.3 * l)
        lam = (jnp.exp(jnp.sum(lam_q1[l].astype(F32) * lam_k1[l].astype(F32)))
               - jnp.exp(jnp.sum(lam_q2[l].astype(F32) * lam_k2[l].astype(F32))) + lam_init)
        wts = (norm1_g[l][None, :], _pack_w_in(w_in[l]),
               jnp.pad(b_forget[l].astype(F32), (0, LANES - FOX_HEADS))[None, :],
               row(fox_q_g[l], FOX_HEADS), row(fox_k_g[l], FOX_HEADS),
               row(diff_q_g[l], 2 * DIFF_HEADS), row(diff_k_g[l], 2 * DIFF_HEADS), ones64, ones32,
               jnp.full((1, LANES), lam, F32), row(diff_sub_g[l], 2),
               (conv_w[l].astype(F32), conv_b[l][None, :], conv_ln_g[l][None, :], conv_ln_b[l][None, :]),
               w_out[l].astype(BF16), norm2_g[l][None, :],
               peer_w_q[l].astype(BF16), peer_k1[l].astype(BF16), peer_k2[l].astype(BF16),
               peer_u[l].astype(BF16), _transpose_cast(peer_v, l, BF16))
        xp, stp = _layer_group(xp, None, tabs_p, wts, lam_init, bp, tp)
        hist = (cache_fox_k[l], cache_fox_v[l], cache_fox_logf[l], cache_diff_k[l], cache_diff_v[l],
                state_conv[l])
        xs, sts = _layer_group(xs, hist, tabs_s, wts, lam_init, bs, ts)
        for i in range(6):
            p_st[i].append(stp[i])
            s_st[i].append(sts[i])

    y_p = xp.reshape(bp, tp, d)
    y_s = xs.reshape(bs, ts, d)
    return (y_p, y_s) + tuple(jnp.stack(a) for a in p_st) + tuple(jnp.stack(a) for a in s_st)
```

```python
import functools
import math

import numpy as np
import jax
import jax.numpy as jnp
from jax import lax
from jax.experimental import pallas as pl
from jax.experimental.pallas import tpu as pltpu

F32 = jnp.float32
BF16 = jnp.bfloat16

D_MODEL = 1024
CHUNK = 64
HEAD_DIM = 64
FOX_HEADS = 6
FOX_WIDTH = FOX_HEADS * HEAD_DIM
CONV_CH = 256
CONV_WIDTH = 31
DIFF_HEADS = 6
DIFF_QK_DIM = 32
DIFF_V_DIM = 64
DIFF_WIDTH = DIFF_HEADS * DIFF_V_DIM
ROPE_THETA = 10000.0
PEER_HEADS = 8
PEER_QDIM = 256
N_KEYS = 128
N_EXPERTS = N_KEYS * N_KEYS
PEER_TOPK = 16
EPS = 1e-6
NEG = -1e30

LANES = 128
SUBLANES = 8
PACK = 16
HEAD_PAIRS = FOX_WIDTH // LANES
ROW_TILE = 512
ATTN_Q_BLOCK = 256
EXPERT_BLOCK = SUBLANES * N_KEYS
VMEM_LIMIT = 56 * 1024 * 1024

_SEG_FQ, _SEG_FK, _SEG_FV = (0, 0), (0, 384), (0, 768)
_SEG_GA, _SEG_GB = (1, 0), (1, 256)
_SEG_DQ, _SEG_DK, _SEG_DV = (1, 512), (1, 896), (1, 1280)
_SEG_FF = (2, 0)
_FF_LO, _FF_HI = 3 * FOX_WIDTH, 3 * FOX_WIDTH + FOX_HEADS

_NT = (((1,), (1,)), ((), ()))


def _cparams(sem, flags=None):
    return pltpu.CompilerParams(dimension_semantics=sem, vmem_limit_bytes=VMEM_LIMIT, flags=flags)


def _group_mean_sq(y, ones_ref, inv_n):
    y2 = y * y
    hi = y2.astype(BF16)
    lo = (y2 - hi.astype(F32)).astype(BF16)
    s = jnp.dot(hi, ones_ref[...], preferred_element_type=F32)
    s = s + jnp.dot(lo, ones_ref[...], preferred_element_type=F32)
    return s * inv_n


def _rope(y, cos, sin_signed):
    outs = []
    for c in range(y.shape[1] // LANES):
        sl = slice(c * LANES, (c + 1) * LANES)
        yc = y[:, sl]
        fwd = pltpu.roll(yc, LANES - DIFF_QK_DIM // 2, 1)
        bwd = pltpu.roll(yc, DIFF_QK_DIM // 2, 1)
        lane = lax.broadcasted_iota(jnp.int32, yc.shape, 1)
        partner = jnp.where((lane % DIFF_QK_DIM) < DIFF_QK_DIM // 2, fwd, bwd)
        outs.append(yc * cos[:, sl] + partner * sin_signed[:, sl])
    return jnp.concatenate(outs, axis=1)


def _inproj_kernel(x_ref, g1_ref, wa_ref, wb_ref, wf_ref, bf_ref, gq_ref, gk_ref, dgq_ref, dgk_ref, cos_ref, sin_ref,
                   ones64_ref, ones32_ref,
                   fq_o, fk_o, fkb_o, fv_o, fvb_o, lf_o, u_o, dq_o, dk_o, dkb_o, dv_o, dvb_o):
    x = x_ref[...]
    ms = jnp.mean(x * x, axis=-1, keepdims=True)
    h = (x * lax.rsqrt(ms + EPS) * g1_ref[...]).astype(BF16)

    w_refs = (wa_ref, wb_ref, wf_ref)

    def seg(where, width):
        piece, lo = where
        return jnp.dot(h, w_refs[piece][:, lo:lo + width].astype(BF16), preferred_element_type=F32)

    fq = seg(_SEG_FQ, FOX_WIDTH)
    fq = fq * lax.rsqrt(_group_mean_sq(fq, ones64_ref, 1.0 / HEAD_DIM) + EPS) * gq_ref[...]
    fq_o[...] = (fq * (HEAD_DIM ** -0.5)).astype(BF16)

    fk = seg(_SEG_FK, FOX_WIDTH)
    fk = fk * lax.rsqrt(_group_mean_sq(fk, ones64_ref, 1.0 / HEAD_DIM) + EPS) * gk_ref[...]
    fk_o[...] = fk
    fkb_o[...] = fk.astype(BF16)

    fv = seg(_SEG_FV, FOX_WIDTH)
    fv_o[...] = fv
    fvb_o[...] = fv.astype(BF16)

    z = seg(_SEG_FF, LANES) + bf_ref[...]
    lf_o[...] = jnp.minimum(z, 0.0) - jnp.log1p(jnp.exp(-jnp.abs(z)))

    ga = seg(_SEG_GA, CONV_CH)
    gb = seg(_SEG_GB, CONV_CH)
    u_o[...] = ga * jax.nn.sigmoid(gb)

    cos = cos_ref[...]
    sin = sin_ref[...]
    dq = seg(_SEG_DQ, DIFF_WIDTH)
    dq = dq * lax.rsqrt(_group_mean_sq(dq, ones32_ref, 1.0 / DIFF_QK_DIM) + EPS) * dgq_ref[...]
    dq_o[...] = (_rope(dq, cos, sin) * (DIFF_QK_DIM ** -0.5)).astype(BF16)

    dk = seg(_SEG_DK, DIFF_WIDTH)
    dk = dk * lax.rsqrt(_group_mean_sq(dk, ones32_ref, 1.0 / DIFF_QK_DIM) + EPS) * dgk_ref[...]
    dk = _rope(dk, cos, sin)
    dk_o[...] = dk
    dkb_o[...] = dk.astype(BF16)

    dv = seg(_SEG_DV, DIFF_WIDTH)
    dv_o[...] = dv
    dvb_o[...] = dv.astype(BF16)


def _inproj(x, g1, w_pad, bf_pad, gq, gk, dgq, dgk, cos_tab, sin_tab, ones64, ones32):
    n = x.shape[0]
    tm = ROW_TILE
    pos_tiles = cos_tab.shape[0] // tm
    row = lambda w: pl.BlockSpec((tm, w), lambda i: (i, 0))
    full = lambda a: pl.BlockSpec(a.shape, lambda i: (0, 0))
    tab = pl.BlockSpec((tm, DIFF_WIDTH), lambda i: (i % pos_tiles, 0))
    w3 = FOX_WIDTH
    out_shape = (
        jax.ShapeDtypeStruct((n, w3), BF16),
        jax.ShapeDtypeStruct((n, w3), F32),
        jax.ShapeDtypeStruct((n, w3), BF16),
        jax.ShapeDtypeStruct((n, w3), F32),
        jax.ShapeDtypeStruct((n, w3), BF16),
        jax.ShapeDtypeStruct((n, LANES), F32),
        jax.ShapeDtypeStruct((n, CONV_CH), F32),
        jax.ShapeDtypeStruct((n, w3), BF16),
        jax.ShapeDtypeStruct((n, w3), F32),
        jax.ShapeDtypeStruct((n, w3), BF16),
        jax.ShapeDtypeStruct((n, w3), F32),
        jax.ShapeDtypeStruct((n, w3), BF16),
    )
    out_specs = (row(w3), row(w3), row(w3), row(w3), row(w3), row(LANES), row(CONV_CH),
                 row(w3), row(w3), row(w3), row(w3), row(w3))
    return pl.pallas_call(
        _inproj_kernel,
        grid=(n // tm,),
        in_specs=[row(D_MODEL), full(g1), full(w_pad[0]), full(w_pad[1]), full(w_pad[2]), full(bf_pad),
                  full(gq), full(gk), full(dgq),
                  full(dgk), tab, tab, full(ones64), full(ones32)],
        out_specs=out_specs,
        out_shape=out_shape,
        compiler_params=_cparams(("parallel",)),
        name="inproj",
    )(x, g1, *w_pad, bf_pad, gq, gk, dgq, dgk, cos_tab, sin_tab, ones64, ones32)


def _softmax_pv(s_groups, v_parts, tq):
    p_groups, l_groups = [], []
    for s_parts in s_groups:
        m = None
        for s in s_parts:
            mi = jnp.max(s, axis=-1, keepdims=True)
            m = mi if m is None else jnp.maximum(m, mi)
        ps, l = [], None
        for s in s_parts:
            p = jnp.exp(s - m)
            li = jnp.sum(p, axis=-1, keepdims=True)
            l = li if l is None else l + li
            ps.append(p.astype(BF16))
        p_groups.append(ps)
        l_groups.append(l)
    acc = None
    for part, v in enumerate(v_parts):
        p_all = jnp.concatenate([ps[part] for ps in p_groups], axis=0)
        ai = jnp.dot(p_all, v, preferred_element_type=F32)
        acc = ai if acc is None else acc + ai
    return [acc[g * tq:(g + 1) * tq] / l for g, l in enumerate(l_groups)]


def _fox_kernel(q_ref, kp_ref, vp_ref, kn_ref, vn_ref, ck_ref, o_ref, *, tq, n_q, past0):
    lane = lax.broadcasted_iota(jnp.int32, (tq, LANES), 1)
    first = lane < HEAD_DIM
    causal = (lax.broadcasted_iota(jnp.int32, (tq, tq), 1) <= lax.broadcasted_iota(jnp.int32, (tq, tq), 0))
    for qi in range(n_q):
        past = past0 + qi * tq
        rows = slice(qi * tq, (qi + 1) * tq)
        q = q_ref[0, rows, :]
        kd = kn_ref[0, rows, :].astype(BF16)
        vd = vn_ref[0, rows, :].astype(BF16)
        if past:
            kp = kp_ref[0, 0:past, :].astype(BF16)
            vp = vp_ref[0, 0:past, :].astype(BF16)
        zq = jnp.zeros_like(q)
        q2 = jnp.concatenate([jnp.where(first, q, zq), jnp.where(first, zq, q)], axis=0)
        s_groups = [[], []]
        v_parts = []
        if past:
            s = lax.dot_general(q2, kp, _NT, preferred_element_type=F32)
            for j in range(2):
                s_groups[j].append(s[j * tq:(j + 1) * tq] - ck_ref[0, 0, j:j + 1, 0:past])
            v_parts.append(vp)
        s = lax.dot_general(q2, kd, _NT, preferred_element_type=F32)
        for j in range(2):
            sj = s[j * tq:(j + 1) * tq] - ck_ref[0, 0, j:j + 1, past:past + tq]
            s_groups[j].append(jnp.where(causal, sj, NEG))
        v_parts.append(vd)
        outs = _softmax_pv(s_groups, v_parts, tq)
        o_ref[0, rows, :] = jnp.where(first, outs[0], outs[1]).astype(o_ref.dtype)


def _diff_kernel(q_ref, kp_ref, vp_ref, kn_ref, vn_ref, lam_ref, g_ref, o_ref, *, tq, n_q, past0, out_scale):
    lane = lax.broadcasted_iota(jnp.int32, (tq, LANES), 1)
    first = lane < DIFF_V_DIM
    chunk_ok = (lax.broadcasted_iota(jnp.int32, (tq, tq), 1) // CHUNK
                <= lax.broadcasted_iota(jnp.int32, (tq, tq), 0) // CHUNK)
    lam = lam_ref[...]
    for qi in range(n_q):
        past = past0 + qi * tq
        rows = slice(qi * tq, (qi + 1) * tq)
        q = q_ref[0, rows, :]
        kd = kn_ref[0, rows, :].astype(BF16)
        vd = vn_ref[0, rows, :].astype(BF16)
        if past:
            kp = kp_ref[0, 0:past, :].astype(BF16)
            vp = vp_ref[0, 0:past, :].astype(BF16)
        zq = jnp.zeros_like(q)
        qms = []
        for g in range(4):
            lo = g * DIFF_QK_DIM
            sel = jnp.where(lane >= lo, lane, LANES) < lo + DIFF_QK_DIM
            qms.append(jnp.where(sel, q, zq))
        q4 = jnp.concatenate(qms, axis=0)
        s_groups = [[] for _ in range(4)]
        v_parts = []
        if past:
            s = lax.dot_general(q4, kp, _NT, preferred_element_type=F32)
            for g in range(4):
                s_groups[g].append(s[g * tq:(g + 1) * tq])
            v_parts.append(vp)
        s = lax.dot_general(q4, kd, _NT, preferred_element_type=F32)
        for g in range(4):
            s_groups[g].append(jnp.where(chunk_ok, s[g * tq:(g + 1) * tq], NEG))
        v_parts.append(vd)
        maps = _softmax_pv(s_groups, v_parts, tq)
        outs = [maps[0] - lam * maps[1], maps[2] - lam * maps[3]]
        o = jnp.where(first, outs[0], outs[1])
        o2 = o * o
        ss0 = jnp.sum(jnp.where(first, o2, 0.0), axis=-1, keepdims=True)
        ss1 = jnp.sum(jnp.where(first, 0.0, o2), axis=-1, keepdims=True)
        ms = jnp.where(first, ss0, ss1) * (1.0 / DIFF_V_DIM)
        o = o * lax.rsqrt(ms + EPS) * g_ref[...]
        o_ref[0, rows, :] = (o * out_scale).astype(o_ref.dtype)


def _cumsum_kernel(lf_ref, tri_ref, o_ref, *, t_total):
    carry = jnp.zeros((SUBLANES, 1), F32)
    tri = tri_ref[...]
    for blk in range(t_total // LANES):
        cols = slice(blk * LANES, (blk + 1) * LANES)
        x8 = lf_ref[0, cols, :].T[0:SUBLANES, :]
        hi = x8.astype(BF16)
        r1 = x8 - hi.astype(F32)
        mid = r1.astype(BF16)
        lo = (r1 - mid.astype(F32)).astype(BF16)
        c = (jnp.dot(hi, tri, preferred_element_type=F32) + jnp.dot(mid, tri, preferred_element_type=F32)
             + jnp.dot(lo, tri, preferred_element_type=F32)) + carry
        o_ref[0, :, cols] = c
        carry = c[:, LANES - 1:LANES]


def _cumsum_time(lf):
    pad = -lf.shape[1] % LANES
    if pad:
        lf = jnp.pad(lf, ((0, 0), (0, pad), (0, 0)))
    b, t_total, _ = lf.shape
    idx = np.arange(LANES)
    tri = jnp.asarray((idx[:, None] <= idx[None, :]).astype(np.float32), dtype=BF16)
    kern = functools.partial(_cumsum_kernel, t_total=t_total)
    return pl.pallas_call(
        kern,
        grid=(b,),
        in_specs=[pl.BlockSpec((1, t_total, LANES), lambda i: (i, 0, 0)),
                  pl.BlockSpec((LANES, LANES), lambda i: (0, 0))],
        out_specs=pl.BlockSpec((1, SUBLANES, t_total), lambda i: (i, 0, 0)),
        out_shape=jax.ShapeDtypeStruct((b, SUBLANES, t_total), F32),
        compiler_params=_cparams(("parallel",)),
        name="cumsum_time",
    )(lf, tri)


def _attn_specs(q, k_past, k_new):
    b, tq_total, _ = q.shape
    tq = min(ATTN_Q_BLOCK, tq_total)
    same = k_past is k_new
    past0 = 0 if same else k_past.shape[1]
    blk = lambda t: pl.BlockSpec((1, t, LANES), lambda bi, hp: (bi, 0, hp))
    specs = [blk(tq_total), blk(k_past.shape[1]), blk(k_past.shape[1]), blk(tq_total), blk(tq_total)]
    return b, tq_total, tq, past0, specs, blk(tq_total)


def _fox_attention(q, k_past, v_past, k_new, v_new, c_all):
    b, tq_total, tq, past0, specs, ospec = _attn_specs(q, k_past, k_new)
    tk_total = c_all.shape[2]
    ck = c_all.reshape(b, SUBLANES // 2, 2, tk_total)
    kern = functools.partial(_fox_kernel, tq=tq, n_q=tq_total // tq, past0=past0)
    return pl.pallas_call(
        kern,
        grid=(b, HEAD_PAIRS),
        in_specs=specs + [pl.BlockSpec((1, 1, 2, tk_total), lambda bi, hp: (bi, hp, 0, 0))],
        out_specs=ospec,
        out_shape=jax.ShapeDtypeStruct((b, tq_total, FOX_WIDTH), BF16),
        compiler_params=_cparams(("parallel", "parallel")),
        name="fox_attn",
    )(q, k_past, v_past, k_new, v_new, ck)


def _diff_attention(q, k_past, v_past, k_new, v_new, lam_row, g_row, out_scale):
    b, tq_total, tq, past0, specs, ospec = _attn_specs(q, k_past, k_new)
    kern = functools.partial(_diff_kernel, tq=tq, n_q=tq_total // tq, past0=past0, out_scale=out_scale)
    vec = pl.BlockSpec((1, LANES), lambda bi, hp: (0, 0))
    return pl.pallas_call(
        kern,
        grid=(b, HEAD_PAIRS),
        in_specs=specs + [vec, vec],
        out_specs=ospec,
        out_shape=jax.ShapeDtypeStruct((b, tq_total, DIFF_WIDTH), BF16),
        compiler_params=_cparams(("parallel", "parallel")),
        name="diff_attn",
    )(q, k_past, v_past, k_new, v_new, lam_row, g_row)


def _conv_kernel(hist_ref, u_ref, w_ref, b_ref, g_ref, beta_ref, o_ref, win_scr, *, t_total, tc):
    pad = _CONV_ALIGN - (CONV_WIDTH - 1)
    win_scr[pad:_CONV_ALIGN, :] = hist_ref[0]
    win_scr[_CONV_ALIGN:_CONV_ALIGN + t_total, :] = u_ref[0]
    for c in range(t_total // tc):
        t0 = c * tc
        acc = jnp.zeros((tc, CONV_CH), F32)
        for w in range(CONV_WIDTH):
            acc = acc + win_scr[pad + t0 + w:pad + t0 + w + tc, :] * w_ref[w:w + 1, :]
        y = acc + b_ref[...]
        mu = jnp.mean(y, axis=-1, keepdims=True)
        yc = y - mu
        var = jnp.mean(yc * yc, axis=-1, keepdims=True)
        yn = yc * lax.rsqrt(var + EPS) * g_ref[...] + beta_ref[...]
        o_ref[0, t0:t0 + tc, :] = (yn * jax.nn.sigmoid(yn)).astype(o_ref.dtype)


_CONV_ALIGN = 32


def _conv(hist, u, w, b, g, beta):
    bsz, t_total, _ = u.shape
    tc = min(128, t_total)
    kern = functools.partial(_conv_kernel, t_total=t_total, tc=tc)
    full = lambda a: pl.BlockSpec(a.shape, lambda i: (0, 0))
    return pl.pallas_call(
        kern,
        grid=(bsz,),
        in_specs=[pl.BlockSpec((1, CONV_WIDTH - 1, CONV_CH), lambda i: (i, 0, 0)),
                  pl.BlockSpec((1, t_total, CONV_CH), lambda i: (i, 0, 0)), full(w), full(b), full(g), full(beta)],
        out_specs=pl.BlockSpec((1, t_total, CONV_CH), lambda i: (i, 0, 0)),
        out_shape=jax.ShapeDtypeStruct((bsz, t_total, CONV_CH), BF16),
        scratch_shapes=[pltpu.VMEM((_CONV_ALIGN + t_total, CONV_CH), F32)],
        compiler_params=_cparams(("parallel",)),
        name="conv",
    )(hist, u, w, b, g, beta)


def _outproj_kernel(x_ref, fo_ref, cy_ref, do_ref, w_ref, g2_ref, x1_o, hn_o, hnt_o):
    y = jnp.dot(fo_ref[...], w_ref[0:FOX_WIDTH, :], preferred_element_type=F32)
    y = y + jnp.dot(cy_ref[...], w_ref[FOX_WIDTH:FOX_WIDTH + CONV_CH, :], preferred_element_type=F32)
    y = y + jnp.dot(do_ref[...], w_ref[FOX_WIDTH + CONV_CH:, :], preferred_element_type=F32)
    x1 = x_ref[...] + y
    x1_o[...] = x1
    ms = jnp.mean(x1 * x1, axis=-1, keepdims=True)
    hn = x1 * lax.rsqrt(ms + EPS) * g2_ref[...]
    hn_o[...] = hn.astype(BF16)
    hnt_o[...] = hn.T.astype(BF16)


def _outproj(x, fo, cy, do, w_out, g2):
    n = x.shape[0]
    tm = ROW_TILE
    row = lambda w: pl.BlockSpec((tm, w), lambda i: (i, 0))
    full = lambda a: pl.BlockSpec(a.shape, lambda i: (0, 0))
    return pl.pallas_call(
        _outproj_kernel,
        grid=(n // tm,),
        in_specs=[row(D_MODEL), row(FOX_WIDTH), row(CONV_CH), row(DIFF_WIDTH), full(w_out), full(g2)],
        out_specs=(row(D_MODEL), row(D_MODEL), pl.BlockSpec((D_MODEL, tm), lambda i: (0, i))),
        out_shape=(jax.ShapeDtypeStruct((n, D_MODEL), F32), jax.ShapeDtypeStruct((n, D_MODEL), BF16),
                   jax.ShapeDtypeStruct((D_MODEL, n), BF16)),
        compiler_params=_cparams(("parallel",)),
        name="outproj",
    )(x, fo, cy, do, w_out, g2)


_STAIR = tuple(PEER_TOPK // (i + 1) for i in range(PEER_TOPK))


def _dup_bf16(x):
    bits = pltpu.bitcast(x.astype(BF16).astype(F32), jnp.uint32)
    return bits | (bits >> 16)


def _peer_route_kernel(hn_ref, wq_ref, k1_ref, k2_ref, cnt_o, a_o, r2_o, b_o, v1_scr, v2_scr):
    tt = hn_ref.shape[0]
    q = jnp.dot(hn_ref[...], wq_ref[...], preferred_element_type=F32).astype(BF16)
    sub = lax.broadcasted_iota(jnp.int32, (SUBLANES, tt), 0)
    for h in range(PEER_HEADS):
        q1 = q[:, h * PEER_QDIM:h * PEER_QDIM + N_KEYS]
        q2 = q[:, h * PEER_QDIM + N_KEYS:(h + 1) * PEER_QDIM]
        s1 = lax.dot_general(k1_ref[...], q1, _NT, preferred_element_type=F32)
        s2 = lax.dot_general(k2_ref[...], q2, _NT, preferred_element_type=F32)

        prev = None
        for r in range(PEER_TOPK):
            cur = jnp.max(s1 if prev is None else jnp.where(s1 < prev, s1, -jnp.inf), axis=0, keepdims=True)
            v1_scr[r:r + 1, :] = cur
            prev = cur
        prev = None
        rank2 = jnp.zeros_like(s2)
        for r in range(PEER_TOPK):
            if prev is None:
                cur = jnp.max(s2, axis=0, keepdims=True)
            else:
                below = s2 < prev
                rank2 = rank2 + jnp.where(below, 1.0, 0.0)
                cur = jnp.max(jnp.where(below, s2, -jnp.inf), axis=0, keepdims=True)
            v2_scr[r:r + 1, :] = cur
            prev = cur
        rank2 = rank2 + jnp.where(s2 < prev, 1.0, 0.0)

        v2lo = v2_scr[0:8, :]
        cands = [v1_scr[0:1, :] + v2lo, v1_scr[0:1, :] + v2_scr[8:16, :]]
        for i in range(1, 8):
            cands.append(jnp.where(sub < _STAIR[i], v1_scr[i:i + 1, :] + v2lo, -jnp.inf))
        cands.append(v1_scr[8:16, :] + v2_scr[0:1, :])
        prev = None
        for r in range(PEER_TOPK):
            best = None
            for c in cands:
                cm = c if prev is None else jnp.where(c < prev, c, -jnp.inf)
                best = cm if best is None else jnp.maximum(best, cm)
            prev = jnp.max(best, axis=0, keepdims=True)
        t16 = prev
        top = v1_scr[0:1, :] + v2_scr[0:1, :]
        z = None
        for c in cands:
            e = jnp.sum(jnp.where(c >= t16, jnp.exp(c - top), 0.0), axis=0, keepdims=True)
            z = e if z is None else z + e

        cnt = jnp.zeros_like(s1)
        for j in range(PEER_TOPK):
            cnt = jnp.where(s1 + v2_scr[j:j + 1, :] >= t16, float(j + 1), cnt)

        rows = slice(h * N_KEYS, (h + 1) * N_KEYS)
        cnt_o[rows, :] = _dup_bf16(cnt)
        a_o[rows, :] = _dup_bf16(jnp.exp(s1 - v1_scr[0:1, :]) * (0.5 / z))
        r2_o[rows, :] = rank2
        b_o[rows, :] = jnp.exp(s2 - v2_scr[0:1, :])


def _peer_route(hn, wq, k1, k2):
    n = hn.shape[0]
    tt = ROW_TILE
    rows = PEER_HEADS * N_KEYS
    full = lambda a: pl.BlockSpec(a.shape, lambda i: (0, 0))
    col = pl.BlockSpec((rows, tt), lambda i: (0, i))
    return pl.pallas_call(
        _peer_route_kernel,
        grid=(n // tt,),
        in_specs=[pl.BlockSpec((tt, D_MODEL), lambda i: (i, 0)), full(wq), full(k1), full(k2)],
        out_specs=(col, col, col, col),
        out_shape=(jax.ShapeDtypeStruct((rows, n), jnp.uint32), jax.ShapeDtypeStruct((rows, n), jnp.uint32),
                   jax.ShapeDtypeStruct((rows, n), F32), jax.ShapeDtypeStruct((rows, n), F32)),
        scratch_shapes=[pltpu.VMEM((PEER_TOPK, tt), F32), pltpu.VMEM((PEER_TOPK, tt), F32)],
        compiler_params=_cparams(("parallel",)),
        name="peer_route",
    )(hn, wq, k1, k2)


_GELU_C = math.sqrt(2.0 / math.pi)


def _peer_dense_kernel(hnt_ref, x1_ref, cnt_ref, a_ref, r2_ref, b_ref, u_ref, vt_ref, o_ref,
                       acc_scr, p_scr, rb_scr, hnt_scr):
    e = pl.program_id(1)
    tt = hnt_ref.shape[1]
    n_tg = tt // LANES
    halves = 2
    hrows = EXPERT_BLOCK // halves
    keys_per_half = hrows // N_KEYS
    n_pk = N_KEYS // PACK
    zero_pk = jnp.zeros((PACK, LANES), BF16)
    n_chunks = PEER_HEADS * n_pk

    @pl.when(e == 0)
    def _():
        acc_scr[...] = jnp.zeros_like(acc_scr)
        hnt_scr[...] = hnt_ref[...]
        for tg in range(n_tg):
            cs = slice(tg * LANES, (tg + 1) * LANES)
            for c in range(n_chunks):
                src = slice(c * PACK, (c + 1) * PACK)
                rb_scr[tg, 2 * c * PACK:(2 * c + 1) * PACK, :] = r2_ref[src, cs].astype(BF16)
                rb_scr[tg, (2 * c + 1) * PACK:(2 * c + 2) * PACK, :] = b_ref[src, cs].astype(BF16)

    total = None
    for hf in range(halves):
        rows_h = slice(hf * hrows, (hf + 1) * hrows)
        ht = jnp.dot(u_ref[rows_h, :], hnt_scr[...], preferred_element_type=F32)
        for il in range(keys_per_half):
            i = hf * keys_per_half + il
            for tg in range(n_tg):
                cs = slice(tg * LANES, (tg + 1) * LANES)
                w = [None] * n_pk
                for h in range(PEER_HEADS):
                    cnt8 = cnt_ref[h, 0, :, cs]
                    a8 = a_ref[h, 0, :, cs]
                    crow = pltpu.bitcast(jnp.broadcast_to(cnt8[i:i + 1, :], (SUBLANES, LANES)), BF16)
                    arow = pltpu.bitcast(jnp.broadcast_to(a8[i:i + 1, :], (SUBLANES, LANES)), BF16)
                    for k in range(n_pk):
                        c = h * n_pk + k
                        r2c = rb_scr[tg, 2 * c * PACK:(2 * c + 1) * PACK, :]
                        bc = rb_scr[tg, (2 * c + 1) * PACK:(2 * c + 2) * PACK, :]
                        term = jnp.where(r2c < crow, bc, zero_pk) * arow
                        w[k] = term if w[k] is None else w[k] + term
                for k in range(n_pk):
                    rk = slice(i * N_KEYS + k * PACK, i * N_KEYS + (k + 1) * PACK)
                    x = ht[il * N_KEYS + k * PACK:il * N_KEYS + (k + 1) * PACK, cs]
                    t = jnp.tanh(x * (_GELU_C + (_GELU_C * 0.044715) * (x * x)))
                    g = (x + x * t).astype(BF16)
                    p_scr[tg, rk, :] = w[k] * g
        p_half = jnp.concatenate([p_scr[tg, rows_h, :] for tg in range(n_tg)], axis=1)
        part = jnp.dot(vt_ref[:, rows_h], p_half, preferred_element_type=F32)
        total = part if total is None else total + part
    acc_scr[...] += total

    @pl.when(e == pl.num_programs(1) - 1)
    def _():
        o_ref[...] = x1_ref[...] + acc_scr[...].T


def _transpose_cast_kernel(v_ref, o_ref):
    o_ref[...] = v_ref[0].T.astype(o_ref.dtype)


def _transpose_cast(v, layer, dtype):
    _, r, c = v.shape
    tr = ROW_TILE
    return pl.pallas_call(
        _transpose_cast_kernel,
        grid=(r // tr,),
        in_specs=[pl.BlockSpec((1, tr, c), lambda i: (layer, i, 0))],
        out_specs=pl.BlockSpec((c, tr), lambda i: (0, i)),
        out_shape=jax.ShapeDtypeStruct((c, r), dtype),
        compiler_params=_cparams(("parallel",)),
        name="transpose_cast",
    )(v)


def _peer_dense(hnt, x1, cnt, a, r2, b, u_bf, vt_bf):
    n = x1.shape[0]
    tt = ROW_TILE
    eb = EXPERT_BLOCK
    rows = PEER_HEADS * N_KEYS
    col = pl.BlockSpec((rows, tt), lambda i, e: (0, i))
    n_blocks = N_EXPERTS // eb
    cnt = cnt.reshape(PEER_HEADS, n_blocks, SUBLANES, n)
    a = a.reshape(PEER_HEADS, n_blocks, SUBLANES, n)
    key_rows = pl.BlockSpec((PEER_HEADS, 1, SUBLANES, tt), lambda i, e: (0, e, 0, i))
    return pl.pallas_call(
        _peer_dense_kernel,
        grid=(n // tt, n_blocks),
        in_specs=[
            pl.BlockSpec((D_MODEL, tt), lambda i, e: (0, i)),
            pl.BlockSpec((tt, D_MODEL), lambda i, e: (i, 0)),
            key_rows, key_rows, col, col,
            pl.BlockSpec((eb, D_MODEL), lambda i, e: (e, 0)),
            pl.BlockSpec((D_MODEL, eb), lambda i, e: (0, e)),
        ],
        out_specs=pl.BlockSpec((tt, D_MODEL), lambda i, e: (i, 0)),
        out_shape=jax.ShapeDtypeStruct((n, D_MODEL), F32),
        scratch_shapes=[pltpu.VMEM((D_MODEL, tt), F32), pltpu.VMEM((tt // LANES, eb, LANES), BF16),
                        pltpu.VMEM((tt // LANES, 2 * rows, LANES), BF16),
                        pltpu.VMEM((D_MODEL, tt), BF16)],
        compiler_params=_cparams(("parallel", "arbitrary")),
        name="peer_dense",
    )(hnt, x1, cnt, a, r2, b, u_bf, vt_bf)


def _pack_w_in(w):
    ffp = jnp.pad(w[:, _FF_LO:_FF_HI], ((0, 0), (0, LANES - FOX_HEADS)))
    return w[:, :_FF_LO], w[:, _FF_HI:], ffp


def _block_ones(width, group):
    idx = np.arange(width) // group
    return jnp.asarray((idx[:, None] == idx[None, :]).astype(np.float32), dtype=BF16)


def _rope_tables(positions):
    half = DIFF_QK_DIM // 2
    inv = ROPE_THETA ** (-jnp.arange(half, dtype=F32) / half)
    ang = positions.astype(F32)[:, None] * inv
    cos = jnp.cos(ang)
    sin = jnp.sin(ang)
    reps = DIFF_WIDTH // DIFF_QK_DIM
    cos_t = jnp.tile(jnp.concatenate([cos, cos], axis=1), (1, reps))
    sin_t = jnp.tile(jnp.concatenate([-sin, sin], axis=1), (1, reps))
    return cos_t, sin_t


def _layer_group(x, hist, tabs, wts, lam_init, bsz, t):
    (g1, w_pad, bf_pad, gq, gk, dgq, dgk, ones64, ones32, lam_row, gsub_row, conv_args, w_out, g2,
     wq, k1, k2, u_bf, vt_bf) = wts
    (fq, fk, fkb, fv, fvb, lf, u, dq, dk, dkb, dv, dvb) = _inproj(
        x, g1, w_pad, bf_pad, gq, gk, dgq, dgk, tabs[0], tabs[1], ones64, ones32)
    r3 = lambda a: a.reshape(bsz, t, -1)
    lf3 = r3(lf)
    lf6 = lf3[:, :, :FOX_HEADS]
    u3 = r3(u)
    if hist is None:
        c_all = _cumsum_time(lf3)
        kp, vp, dkp, dvp = r3(fkb), r3(fvb), r3(dkb), r3(dvb)
        kn, vn, dkn, dvn = kp, vp, dkp, dvp
        conv_hist = jnp.zeros((bsz, CONV_WIDTH - 1, CONV_CH), F32)
    else:
        pk, pv, plf, pdk, pdv, conv_hist = hist
        past = pk.shape[1]
        plf_pad = jnp.pad(plf.astype(F32), ((0, 0), (0, 0), (0, LANES - FOX_HEADS)))
        c_all = _cumsum_time(jnp.concatenate([plf_pad, lf3], axis=1))
        kp, vp = pk.reshape(bsz, past, FOX_WIDTH), pv.reshape(bsz, past, FOX_WIDTH)
        dkp, dvp = pdk.reshape(bsz, past, DIFF_WIDTH), pdv.reshape(bsz, past, DIFF_WIDTH)
        kn, vn, dkn, dvn = r3(fkb), r3(fvb), r3(dkb), r3(dvb)
        conv_hist = conv_hist.astype(F32)
    fox = _fox_attention(r3(fq), kp, vp, kn, vn, c_all)
    diff = _diff_attention(r3(dq), dkp, dvp, dkn, dvn, lam_row, gsub_row, 1.0 - lam_init)
    cy = _conv(conv_hist, u3, *conv_args)
    flat = lambda a: a.reshape(bsz * t, -1)
    x1, hn, hnt = _outproj(x, flat(fox), flat(cy), flat(diff), w_out, g2)
    cnt, a, r2, b = _peer_route(hn, wq, k1, k2)
    x_new = _peer_dense(hnt, x1, cnt, a, r2, b, u_bf, vt_bf)
    hd = lambda arr, nh: arr.reshape(bsz, t, nh, -1)
    keep = CONV_WIDTH - 1
    new_conv = u3[:, -keep:] if t >= keep else jnp.concatenate([conv_hist, u3], axis=1)[:, -keep:]
    state = (hd(fk, FOX_HEADS), hd(fv, FOX_HEADS), lf6, hd(dk, DIFF_HEADS), hd(dv, DIFF_HEADS), new_conv)
    return x_new, state


def kernel(x_prompt, x_sample, cache_fox_k, cache_fox_v, cache_fox_logf, cache_diff_k, cache_diff_v, state_conv, norm1_g, w_in, b_forget, fox_q_g, fox_k_g, diff_q_g, diff_k_g, lam_q1, lam_k1, lam_q2, lam_k2, diff_sub_g, conv_w, conv_b, conv_ln_g, conv_ln_b, w_out, norm2_g, peer_w_q, peer_k1, peer_k2, peer_u, peer_v):
    bp, tp, d = x_prompt.shape
    bs, ts, _ = x_sample.shape
    depth = w_in.shape[0]
    past = cache_fox_k.shape[2]
    assert d == D_MODEL and tp % ROW_TILE == 0 and (bs * ts) % ROW_TILE == 0 and ROW_TILE % ts == 0

    xp = x_prompt.reshape(bp * tp, d)
    xs = x_sample.reshape(bs * ts, d)
    tabs_p = _rope_tables(jnp.arange(tp, dtype=jnp.int32))
    tabs_s = _rope_tables(jnp.tile(past + jnp.arange(ts, dtype=jnp.int32), ROW_TILE // ts))
    ones64 = _block_ones(FOX_WIDTH, HEAD_DIM)
    ones32 = _block_ones(DIFF_WIDTH, DIFF_QK_DIM)
    row = lambda v, reps: jnp.tile(v.astype(F32), reps)[None, :]

    p_st = [[] for _ in range(6)]
    s_st = [[] for _ in range(6)]
    for l in range(depth):
        lam_init = 0.8 - 0.6 * math.exp(-0.3 * l)
        lam = (jnp.exp(jnp.sum(lam_q1[l].astype(F32) * lam_k1[l].astype(F32)))
               - jnp.exp(jnp.sum(lam_q2[l].astype(F32) * lam_k2[l].astype(F32))) + lam_init)
        wts = (norm1_g[l][None, :], _pack_w_in(w_in[l]),
               jnp.pad(b_forget[l].astype(F32), (0, LANES - FOX_HEADS))[None, :],
               row(fox_q_g[l], FOX_HEADS), row(fox_k_g[l], FOX_HEADS),
               row(diff_q_g[l], 2 * DIFF_HEADS), row(diff_k_g[l], 2 * DIFF_HEADS), ones64, ones32,
               jnp.full((1, LANES), lam, F32), row(diff_sub_g[l], 2),
               (conv_w[l].astype(F32), conv_b[l][None, :], conv_ln_g[l][None, :], conv_ln_b[l][None, :]),
               w_out[l].astype(BF16), norm2_g[l][None, :],
               peer_w_q[l].astype(BF16), peer_k1[l].astype(BF16), peer_k2[l].astype(BF16),
               peer_u[l].astype(BF16), _transpose_cast(peer_v, l, BF16))
        xp, stp = _layer_group(xp, None, tabs_p, wts, lam_init, bp, tp)
        hist = (cache_fox_k[l], cache_fox_v[l], cache_fox_logf[l], cache_diff_k[l], cache_diff_v[l],
                state_conv[l])
        xs, sts = _layer_group(xs, hist, tabs_s, wts, lam_init, bs, ts)
        for i in range(6):
            p_st[i].append(stp[i])
            s_st[i].append(sts[i])

    y_p = xp.reshape(bp, tp, d)
    y_s = xs.reshape(bs, ts, d)
    return (y_p, y_s) + tuple(jnp.stack(a) for a in p_st) + tuple(jnp.stack(a) for a in s_st)
```
